```python
import math
import jax, jax.numpy as jnp
from jax import lax
import numpy as np

D_MODEL = 1024
BATCH = 8
SEQ = 4096
DEPTH = 2
DEC_BATCH = 32
DEC_SEQ = 4
PAST_LEN = 16384
PAGE_SIZE = 128

N_EVEN = (DEPTH + 1) // 2
N_ODD = DEPTH // 2
D_CONV = D_MODEL // 2
CONV_W = 3
N_HEADS_B = 8
N_KV_B = 2
HD_B = 64
N_IDX_HEADS = 8
IDX_DIM = 64
TOPK_MAX = 256
Q_BLOCK = 128
N_HEADS_C = 8
DQK_C = 64
DV_C = D_MODEL // N_HEADS_C
MLSTM_CHUNK = 64
N_MEM = 256
N_HEADS_X = 4
HD_X = 128
D_FF = 4 * D_MODEL
ROPE_THETA = 500000.0
NORM_EPS = 1e-6
PAGE_SLACK_NUM = 5
PAGE_SLACK_DEN = 4
EVEN_SIZES = (D_CONV, D_CONV, D_CONV, N_HEADS_B * HD_B, N_KV_B * HD_B, N_KV_B * HD_B, N_IDX_HEADS * IDX_DIM, N_IDX_HEADS, IDX_DIM)
ODD_SIZES = (N_HEADS_C * DQK_C, N_HEADS_C * DQK_C, N_HEADS_C * DV_C, N_HEADS_C * DV_C, N_HEADS_C, N_HEADS_C)
P_EVEN = sum(EVEN_SIZES)
P_ODD = sum(ODD_SIZES)
F32 = jnp.float32

kernel_name = 'hybrid_conv_dsa_mlstm_decoder_step'


def _split(x, sizes):
    offsets = np.cumsum(np.array(sizes))[:-1].tolist()
    return jnp.split(x, offsets, axis=-1)


def rmsnorm(x, g):
    x32 = x.astype(F32)
    y = x32 * lax.rsqrt(jnp.mean(x32 * x32, axis=-1, keepdims=True) + NORM_EPS) * g.astype(F32)
    return y.astype(x.dtype)


def rope(x, pos):
    d = x.shape[-1]
    r = d // 4
    half = r // 2
    freqs = ROPE_THETA ** (-jnp.arange(half, dtype=F32) * 2.0 / r)
    ang = pos.astype(F32)[:, None] * freqs[None, :]
    cos = jnp.cos(ang)[:, None, :]
    sin = jnp.sin(ang)[:, None, :]
    x32 = x.astype(F32)
    x1, x2, rest = x32[..., :half], x32[..., half:r], x32[..., r:]
    out = jnp.concatenate([x1 * cos - x2 * sin, x2 * cos + x1 * sin, rest], axis=-1)
    return out.astype(x.dtype)


def short_conv(z, buf, w):
    t = z.shape[1]
    zp = jnp.concatenate([buf.astype(z.dtype), z], axis=1)
    out = w[0] * zp[:, 0:t]
    for j in range(1, CONV_W):
        out = out + w[j] * zp[:, j:j + t]
    return out, zp[:, t:]


def even_project(xn, w_in, pos):
    bsz, t, _ = xn.shape
    u, gb, gc, q, k, v, qi, wi, ki = _split(xn @ w_in, EVEN_SIZES)
    q = rope(q.reshape(bsz, t, N_HEADS_B, HD_B), pos) * HD_B ** -0.5
    q = q.reshape(bsz, t, N_KV_B, N_HEADS_B // N_KV_B, HD_B)
    k = rope(k.reshape(bsz, t, N_KV_B, HD_B), pos)
    v = v.reshape(bsz, t, N_KV_B, HD_B)
    qi = rope(qi.reshape(bsz, t, N_IDX_HEADS, IDX_DIM), pos) * IDX_DIM ** -0.5
    ki = rope(ki.reshape(bsz, t, 1, IDX_DIM), pos)[:, :, 0]
    wi = wi * N_IDX_HEADS ** -0.5
    return gc * u, gb, q, k, v, qi, wi, ki


def indexer_select(qi, wi, ki, q_pos, k_top):
    s = jnp.einsum('bthd,bld->bthl', qi.astype(F32), ki.astype(F32))
    score = jnp.einsum('bth,bthl->btl', wi.astype(F32), jax.nn.relu(s))
    admissible = jnp.arange(ki.shape[1])[None, :] <= q_pos[:, None]
    score = jnp.where(admissible[None], score, -jnp.inf)
    _, idx = lax.top_k(score, k_top)
    valid = idx <= q_pos[None, :, None]
    return idx, valid


def sparse_attend(q, k_sel, v_sel, valid):
    s = jnp.einsum('bthgd,btnhd->bthgn', q.astype(F32), k_sel.astype(F32))
    s = jnp.where(valid[:, :, None, None, :], s, -jnp.inf)
    p = jax.nn.softmax(s, axis=-1)
    o = jnp.einsum('bthgn,btnhd->bthgd', p, v_sel.astype(F32))
    return o.reshape(q.shape[0], q.shape[1], -1).astype(q.dtype)


def even_mixer_prompt(xn, w_in, conv_w, w_out):
    bsz, t, _ = xn.shape
    pos = jnp.arange(t)
    z, gb, q, k, v, qi, wi, ki = even_project(xn, w_in, pos)
    conv_out, conv_buf = short_conv(z, jnp.zeros((bsz, CONV_W - 1, D_CONV), z.dtype), conv_w)
    y_a = gb * conv_out
    k_top = min(TOPK_MAX, t // 4)
    qb = min(Q_BLOCK, t)
    bidx = jnp.arange(bsz)[:, None, None]

    def block(i):
        start = i * qb
        sl = lambda a: lax.dynamic_slice_in_dim(a, start, qb, axis=1)
        q_pos = start + jnp.arange(qb)
        idx, valid = indexer_select(sl(qi), sl(wi), ki, q_pos, k_top)
        return sparse_attend(sl(q), k[bidx, idx], v[bidx, idx], valid)

    y_b = lax.map(block, jnp.arange(t // qb))
    y_b = jnp.moveaxis(y_b, 0, 1).reshape(bsz, t, N_HEADS_B * HD_B)
    y = jnp.concatenate([y_a, y_b], axis=-1) @ w_out
    return y, (k, v, ki, conv_buf)


def even_mixer_sample(xn, w_in, conv_w, w_out, k_pool, v_pool, ik_pool, conv_buf, page_table):
    bsz, t, _ = xn.shape
    n_pages = page_table.shape[1]
    past = n_pages * PAGE_SIZE
    pos = past + jnp.arange(t)
    z, gb, q, k, v, qi, wi, ki = even_project(xn, w_in, pos)
    conv_out, new_buf = short_conv(z, conv_buf, conv_w)
    y_a = gb * conv_out
    ki_past = ik_pool[page_table].reshape(bsz, past, IDX_DIM).astype(ki.dtype)
    ki_all = jnp.concatenate([ki_past, ki], axis=1)
    k_top = min(TOPK_MAX, (past + t) // 4)
    idx, valid = indexer_select(qi, wi, ki_all, pos, k_top)
    bidx = jnp.arange(bsz)[:, None, None]
    in_past = (idx < past)[..., None, None]
    phys_page = page_table[bidx, jnp.minimum(idx // PAGE_SIZE, n_pages - 1)]
    row = phys_page * PAGE_SIZE + idx % PAGE_SIZE
    j_new = jnp.clip(idx - past, 0, t - 1)
    k_rows = k_pool.reshape(-1, N_KV_B, HD_B)
    v_rows = v_pool.reshape(-1, N_KV_B, HD_B)
    k_sel = jnp.where(in_past, k_rows[row].astype(k.dtype), k[bidx, j_new])
    v_sel = jnp.where(in_past, v_rows[row].astype(v.dtype), v[bidx, j_new])
    y_b = sparse_attend(q, k_sel, v_sel, valid)
    y = jnp.concatenate([y_a, y_b], axis=-1) @ w_out
    return y, (k, v, ki, new_buf)


def mlstm_chunkwise(q, k, v, ig, lf, c0, n0, m0):
    bsz, nh, t, _ = q.shape
    dv = v.shape[-1]
    lc = math.gcd(t, MLSTM_CHUNK)
    nc = t // lc

    def chunks(a):
        return jnp.moveaxis(a.reshape(bsz, nh, nc, lc, *a.shape[3:]), 2, 0)

    causal = jnp.tril(jnp.ones((lc, lc), dtype=bool))

    def step(carry, xs):
        c, n, m = carry
        qc, kc, vc, ic, fc = xs
        b = jnp.cumsum(fc, axis=-1)
        a = b + m[..., None]
        d = jnp.where(causal, b[..., :, None] - b[..., None, :] + ic[..., None, :], -jnp.inf)
        mj = jnp.maximum(a, jnp.max(d, axis=-1))
        s = jnp.einsum('bhld,bhsd->bhls', qc, kc) * jnp.exp(d - mj[..., None])
        aw = jnp.exp(a - mj)
        num = jnp.einsum('bhls,bhsv->bhlv', s, vc) + aw[..., None] * jnp.einsum('bhvd,bhld->bhlv', c, qc)
        den = jnp.sum(s, axis=-1) + aw * jnp.einsum('bhd,bhld->bhl', n, qc)
        h = num / jnp.maximum(jnp.abs(den), jnp.exp(-mj))[..., None]
        b_last = b[..., -1]
        g = b_last[..., None] - b + ic
        m_new = jnp.maximum(b_last + m, jnp.max(g, axis=-1))
        gw = jnp.exp(g - m_new[..., None])
        decay = jnp.exp(b_last + m - m_new)
        c_new = decay[..., None, None] * c + jnp.einsum('bhs,bhsv,bhsd->bhvd', gw, vc, kc)
        n_new = decay[..., None] * n + jnp.einsum('bhs,bhsd->bhd', gw, kc)
        return (c_new, n_new, m_new), h

    (c, n, m), hs = lax.scan(step, (c0, n0, m0), (chunks(q), chunks(k), chunks(v), chunks(ig), chunks(lf)))
    h = jnp.moveaxis(hs, 0, 2).reshape(bsz, nh, t, dv)
    return h, (c, n, m)


def odd_mixer(xn, w_in, b_i, b_f, hn_g, w_out, c0, n0, m0):
    bsz, t, _ = xn.shape
    q, k, v, o, ig, fg = _split(xn @ w_in, ODD_SIZES)

    def heads(a, d):
        return jnp.swapaxes(a.reshape(bsz, t, N_HEADS_C, d), 1, 2).astype(F32)

    q = heads(q, DQK_C) * DQK_C ** -0.5
    k = heads(k, DQK_C)
    v = heads(v, DV_C)
    ig = jnp.swapaxes((ig + b_i).astype(F32), 1, 2)
    lf = jax.nn.log_sigmoid(jnp.swapaxes((fg + b_f).astype(F32), 1, 2))
    h, state = mlstm_chunkwise(q, k, v, ig, lf, c0.astype(F32), n0.astype(F32), m0.astype(F32))
    h = h * lax.rsqrt(jnp.mean(h * h, axis=-1, keepdims=True) + NORM_EPS)
    h = jnp.swapaxes(h, 1, 2).reshape(bsz, t, N_HEADS_C * DV_C) * hn_g.astype(F32)
    y = (h.astype(xn.dtype) * jax.nn.sigmoid(o)) @ w_out
    return y, state


def mem_kv(mem, g, w_k, w_v):
    bsz, n_mem, _ = mem.shape
    mn = rmsnorm(mem, g)
    return (mn @ w_k).reshape(bsz, n_mem, N_HEADS_X, HD_X), (mn @ w_v).reshape(bsz, n_mem, N_HEADS_X, HD_X)


def cross_attend(xn, mk, mv, w_q, w_o):
    bsz, t, _ = xn.shape
    q = (xn @ w_q).reshape(bsz, t, N_HEADS_X, HD_X).astype(F32) * HD_X ** -0.5
    p = jax.nn.softmax(jnp.einsum('bthd,bmhd->bhtm', q, mk.astype(F32)), axis=-1)
    o = jnp.einsum('bhtm,bmhd->bthd', p, mv.astype(F32)).reshape(bsz, t, N_HEADS_X * HD_X)
    return o.astype(xn.dtype) @ w_o


def sq_relu_mlp(xn, w1, w2):
    return jnp.square(jax.nn.relu(xn @ w1)) @ w2


def setup_inputs(seed: int = 0) -> dict:
    key = jax.random.key(seed)
    ks = jax.random.split(key, 32)
    n_pages = PAST_LEN // PAGE_SIZE
    n_phys = (DEC_BATCH * n_pages * PAGE_SLACK_NUM) // PAGE_SLACK_DEN

    def nrm(k, shape, scale=1.0):
        return jax.random.normal(k, shape, F32) * scale

    page_table = jax.random.permutation(ks[0], n_phys)[: DEC_BATCH * n_pages].reshape(DEC_BATCH, n_pages).astype(jnp.int32)
    w_mix_b = D_CONV + N_HEADS_B * HD_B
    return {
        'x_prompt': nrm(ks[1], (BATCH, SEQ, D_MODEL)),
        'x_sample': nrm(ks[2], (DEC_BATCH, DEC_SEQ, D_MODEL)),
        'cache_k': nrm(ks[3], (N_EVEN, n_phys, PAGE_SIZE, N_KV_B, HD_B)),
        'cache_v': nrm(ks[4], (N_EVEN, n_phys, PAGE_SIZE, N_KV_B, HD_B)),
        'cache_idx_k': nrm(ks[5], (N_EVEN, n_phys, PAGE_SIZE, IDX_DIM)),
        'cache_mem_k': nrm(ks[6], (DEPTH, DEC_BATCH, N_MEM, N_HEADS_X, HD_X)),
        'cache_mem_v': nrm(ks[7], (DEPTH, DEC_BATCH, N_MEM, N_HEADS_X, HD_X)),
        'state_conv': nrm(ks[8], (N_EVEN, DEC_BATCH, CONV_W - 1, D_CONV)),
        'state_C': nrm(ks[9], (N_ODD, DEC_BATCH, N_HEADS_C, DV_C, DQK_C), 0.5),
        'state_n': nrm(ks[10], (N_ODD, DEC_BATCH, N_HEADS_C, DQK_C)),
        'state_m': nrm(ks[11], (N_ODD, DEC_BATCH, N_HEADS_C), 0.5),
        'page_table': page_table,
        'mem_prompt': nrm(ks[12], (BATCH, N_MEM, D_MODEL)),
        'norm_g': 1.0 + nrm(ks[13], (DEPTH, 6, D_MODEL), 0.02),
        'w_in_even': nrm(ks[14], (N_EVEN, D_MODEL, P_EVEN), D_MODEL ** -0.5),
        'conv_w': nrm(ks[15], (N_EVEN, CONV_W, D_CONV), CONV_W ** -0.5),
        'w_out_even': nrm(ks[16], (N_EVEN, w_mix_b, D_MODEL), w_mix_b ** -0.5),
        'w_in_odd': nrm(ks[17], (N_ODD, D_MODEL, P_ODD), D_MODEL ** -0.5),
        'b_i': nrm(ks[18], (N_ODD, N_HEADS_C), 0.1),
        'b_f': 3.0 + nrm(ks[19], (N_ODD, N_HEADS_C), 0.5),
        'hnorm_g': 1.0 + nrm(ks[20], (N_ODD, N_HEADS_C * DV_C), 0.02),
        'w_out_odd': nrm(ks[21], (N_ODD, N_HEADS_C * DV_C, D_MODEL), (N_HEADS_C * DV_C) ** -0.5),
        'mem_norm_g': 1.0 + nrm(ks[22], (DEPTH, D_MODEL), 0.02),
        'w_xq': nrm(ks[23], (DEPTH, D_MODEL, N_HEADS_X * HD_X), D_MODEL ** -0.5),
        'w_xk': nrm(ks[24], (DEPTH, D_MODEL, N_HEADS_X * HD_X), D_MODEL ** -0.5),
        'w_xv': nrm(ks[25], (DEPTH, D_MODEL, N_HEADS_X * HD_X), D_MODEL ** -0.5),
        'w_xo': nrm(ks[26], (DEPTH, N_HEADS_X * HD_X, D_MODEL), (N_HEADS_X * HD_X) ** -0.5),
        'w_ff1': nrm(ks[27], (DEPTH, D_MODEL, D_FF), D_MODEL ** -0.5),
        'w_ff2': nrm(ks[28], (DEPTH, D_FF, D_MODEL), D_FF ** -0.5),
    }


def reference(x_prompt, x_sample, cache_k, cache_v, cache_idx_k, cache_mem_k, cache_mem_v, state_conv, state_C, state_n, state_m, page_table, mem_prompt, norm_g, w_in_even, conv_w, w_out_even, w_in_odd, b_i, b_f, hnorm_g, w_out_odd, mem_norm_g, w_xq, w_xk, w_xv, w_xo, w_ff1, w_ff2):
    xp, xs = x_prompt, x_sample
    bp = xp.shape[0]
    kp_l, vp_l, ikp_l, cvp_l, cp_l, np_l, mp_l, mkp_l, mvp_l = [], [], [], [], [], [], [], [], []
    ks_l, vs_l, iks_l, cvs_l, cs_l, ns_l, ms_l = [], [], [], [], [], [], []
    for l in range(DEPTH):
        g = norm_g[l]
        if l % 2 == 0:
            e = l // 2
            yp, (kp, vp, ikp, cvp) = even_mixer_prompt(rmsnorm(xp, g[0]), w_in_even[e], conv_w[e], w_out_even[e])
            ys, (k_s, v_s, ik_s, cv_s) = even_mixer_sample(rmsnorm(xs, g[0]), w_in_even[e], conv_w[e], w_out_even[e],
                                                           cache_k[e], cache_v[e], cache_idx_k[e], state_conv[e], page_table)
            kp_l.append(kp); vp_l.append(vp); ikp_l.append(ikp); cvp_l.append(cvp)
            ks_l.append(k_s); vs_l.append(v_s); iks_l.append(ik_s); cvs_l.append(cv_s)
        else:
            o = l // 2
            c0 = jnp.zeros((bp, N_HEADS_C, DV_C, DQK_C), F32)
            n0 = jnp.zeros((bp, N_HEADS_C, DQK_C), F32)
            m0 = jnp.zeros((bp, N_HEADS_C), F32)
            yp, (c_p1, n_p1, m_p1) = odd_mixer(rmsnorm(xp, g[0]), w_in_odd[o], b_i[o], b_f[o], hnorm_g[o], w_out_odd[o], c0, n0, m0)
            ys, (c_s1, n_s1, m_s1) = odd_mixer(rmsnorm(xs, g[0]), w_in_odd[o], b_i[o], b_f[o], hnorm_g[o], w_out_odd[o],
                                               state_C[o], state_n[o], state_m[o])
            cp_l.append(c_p1); np_l.append(n_p1); mp_l.append(m_p1)
            cs_l.append(c_s1); ns_l.append(n_s1); ms_l.append(m_s1)
        xp = xp + rmsnorm(yp, g[1])
        xs = xs + rmsnorm(ys, g[1])
        mk, mv = mem_kv(mem_prompt, mem_norm_g[l], w_xk[l], w_xv[l])
        mkp_l.append(mk); mvp_l.append(mv)
        xp = xp + rmsnorm(cross_attend(rmsnorm(xp, g[2]), mk, mv, w_xq[l], w_xo[l]), g[3])
        xs = xs + rmsnorm(cross_attend(rmsnorm(xs, g[2]), cache_mem_k[l], cache_mem_v[l], w_xq[l], w_xo[l]), g[3])
        xp = xp + rmsnorm(sq_relu_mlp(rmsnorm(xp, g[4]), w_ff1[l], w_ff2[l]), g[5])
        xs = xs + rmsnorm(sq_relu_mlp(rmsnorm(xs, g[4]), w_ff1[l], w_ff2[l]), g[5])
    k_rows_p = jnp.stack(kp_l)
    v_rows_p = jnp.stack(vp_l)
    idx_k_rows_p = jnp.stack(ikp_l)
    conv_p = jnp.stack(cvp_l)
    c_p = jnp.stack(cp_l)
    n_p = jnp.stack(np_l)
    m_p = jnp.stack(mp_l)
    mem_k_p = jnp.stack(mkp_l)
    mem_v_p = jnp.stack(mvp_l)
    k_rows_s = jnp.stack(ks_l)
    v_rows_s = jnp.stack(vs_l)
    idx_k_rows_s = jnp.stack(iks_l)
    conv_s = jnp.stack(cvs_l)
    c_s = jnp.stack(cs_l)
    n_s = jnp.stack(ns_l)
    m_s = jnp.stack(ms_l)
    return (xp, xs, k_rows_p, v_rows_p, idx_k_rows_p, conv_p, c_p, n_p, m_p, mem_k_p, mem_v_p,
            k_rows_s, v_rows_s, idx_k_rows_s, conv_s, c_s, n_s, m_s)
```

```python
import functools
import math

import jax
import jax.numpy as jnp
from jax import lax
from jax.experimental import pallas as pl
from jax.experimental.pallas import tpu as pltpu

F32 = jnp.float32
BF16 = jnp.bfloat16
I32 = jnp.int32

D_MODEL = 1024
D_CONV = 512
CONV_W = 3
N_HEADS_B = 8
N_KV_B = 2
HD_B = 64
N_IDX_HEADS = 8
IDX_DIM = 64
TOPK_MAX = 256
N_HEADS_C = 8
DQK_C = 64
DV_C = 128
N_MEM = 256
N_HEADS_X = 4
HD_X = 128
D_FF = 4096
PAGE_SIZE = 128
ROPE_THETA = 500000.0
NORM_EPS = 1e-6

LANES = 128
SUBLANES = 8
VMEM_LIMIT = 48 * 1024 * 1024

NEG_BIG = -1e30
INT_MIN = -(2 ** 31)
KEY_NEG_INF = -(2 ** 31) + 0x007FFFFF

_EV_U, _EV_GB, _EV_GC, _EV_Q, _EV_K, _EV_V, _EV_QI, _EV_KI, _EV_END = (
    0, 512, 1024, 1536, 2048, 2176, 2304, 2816, 2944)
_OD_Q, _OD_K, _OD_V, _OD_O, _OD_END = 0, 512, 1024, 2048, 3072


def _cparams(sem, vmem=VMEM_LIMIT):
    return pltpu.CompilerParams(dimension_semantics=sem, vmem_limit_bytes=vmem)


def _rms(x, g):
    return x * lax.rsqrt(jnp.mean(x * x, axis=-1, keepdims=True) + NORM_EPS) * g


def _dot(a, b):
    return jnp.dot(a, b, preferred_element_type=F32)


def _dot_nt(a, b):
    return lax.dot_general(a, b, (((1,), (1,)), ((), ())), preferred_element_type=F32)


def _rope128(x, c, sa, sb):
    return x * c + pltpu.roll(x, LANES - 8, 1) * sa + pltpu.roll(x, 8, 1) * sb


def _rope_tables(pos):
    r = HD_B // 4
    half = r // 2
    freqs = ROPE_THETA ** (-jnp.arange(half, dtype=F32) * 2.0 / r)
    ang = pos.astype(F32)[:, None] * freqs[None, :]
    cos, sin = jnp.cos(ang), jnp.sin(ang)
    t = pos.shape[0]
    ones = jnp.ones((t, HD_B - r), F32)
    zeros = jnp.zeros((t, HD_B - r), F32)
    zh = jnp.zeros((t, half), F32)
    c = jnp.concatenate([cos, cos, ones], axis=1)
    sa = jnp.concatenate([-sin, zh, zeros], axis=1)
    sb = jnp.concatenate([zh, sin, zeros], axis=1)
    tile2 = lambda a: jnp.concatenate([a, a], axis=1)
    return tile2(c), tile2(sa), tile2(sb)


def _even_proj_body(x_ref, g_ref, w_ref, wwi_ref, cw_ref, cos_ref, sa_ref, sb_ref, init_ref,
                    ya_ref, q_ref, k_ref, v_ref, qi_ref, ki_ref, wit_ref, conv_ref, *scratch,
                    tiles_per_seq, time_major_b):
    tm = x_ref.shape[0]
    xn = _rms(x_ref[...], g_ref[...]).astype(BF16)
    c, sa, sb = cos_ref[...], sa_ref[...], sb_ref[...]

    def seg(a, b):
        return _dot(xn, w_ref[:, a:b])

    z = seg(_EV_GC, _EV_Q) * seg(_EV_U, _EV_GB)
    cw = cw_ref[...]
    if time_major_b is None:
        carry_ref, = scratch

        @pl.when(pl.program_id(0) % tiles_per_seq == 0)
        def _():
            carry_ref[0:2, :] = init_ref[0]

        c0 = carry_ref[0:1, :]
        c1 = carry_ref[1:2, :]
        row = lax.broadcasted_iota(I32, (tm, 1), 0)
        z1 = jnp.where(row == 0, c1, pltpu.roll(z, 1, 0))
        z2 = jnp.where(row == 0, c0, jnp.where(row == 1, c1, pltpu.roll(z, 2, 0)))
        carry_ref[0:2, :] = z[tm - 2:tm, :]
        conv_ref[0] = z[tm - 2:tm, :]
    else:
        nb = time_major_b
        init = init_ref[...]
        z1 = jnp.concatenate([init[nb:2 * nb], z[:tm - nb]], axis=0)
        z2 = jnp.concatenate([init, z[:tm - 2 * nb]], axis=0)
        conv_ref[...] = z[tm - 2 * nb:, :]
    conv = cw[0:1] * z2 + cw[1:2] * z1 + cw[2:3] * z
    ya_ref[...] = (seg(_EV_GB, _EV_GC) * conv).astype(BF16)

    for j in range(4):
        a = _EV_Q + j * LANES
        q_ref[:, j * LANES:(j + 1) * LANES] = (
            _rope128(seg(a, a + LANES), c, sa, sb) * HD_B ** -0.5).astype(BF16)
        a = _EV_QI + j * LANES
        qi_ref[:, j * LANES:(j + 1) * LANES] = (
            _rope128(seg(a, a + LANES), c, sa, sb) * IDX_DIM ** -0.5).astype(BF16)
    k_ref[...] = _rope128(seg(_EV_K, _EV_V), c, sa, sb)
    v_ref[...] = seg(_EV_V, _EV_QI)
    ki_ref[...] = _rope128(seg(_EV_KI, _EV_END), c, sa, sb)[:, :IDX_DIM]
    wit_ref[...] = _dot_nt(wwi_ref[...], xn) * N_IDX_HEADS ** -0.5


def _even_proj(x, g, w, wwi, cw, tabs, init, *, tm, seq_len, time_major_b=None):
    m = x.shape[0]
    nt = m // tm
    ntab = tabs[0].shape[0] // tm
    row = lambda i: (i, 0)
    const = lambda i: (0, 0)
    tab_spec = pl.BlockSpec((tm, LANES), lambda i: (i % ntab, 0))
    if time_major_b is None:
        tiles_per_seq = seq_len // tm
        init_spec = pl.BlockSpec((1, 2, D_CONV), lambda i: (i // tiles_per_seq, 0, 0))
        conv_shape = jax.ShapeDtypeStruct((m // seq_len, 2, D_CONV), F32)
        conv_spec = pl.BlockSpec((1, 2, D_CONV), lambda i: (i // tiles_per_seq, 0, 0))
        scratch = [pltpu.VMEM((SUBLANES, D_CONV), F32)]
    else:
        assert nt == 1 and seq_len >= 2
        tiles_per_seq = 1
        init_spec = pl.BlockSpec((2 * time_major_b, D_CONV), const)
        conv_shape = jax.ShapeDtypeStruct((2 * time_major_b, D_CONV), F32)
        conv_spec = pl.BlockSpec((2 * time_major_b, D_CONV), const)
        scratch = []
    out_shape = (
        jax.ShapeDtypeStruct((m, D_CONV), BF16),
        jax.ShapeDtypeStruct((m, N_HEADS_B * HD_B), BF16),
        jax.ShapeDtypeStruct((m, N_KV_B * HD_B), F32),
        jax.ShapeDtypeStruct((m, N_KV_B * HD_B), F32),
        jax.ShapeDtypeStruct((m, N_IDX_HEADS * IDX_DIM), BF16),
        jax.ShapeDtypeStruct((m, IDX_DIM), F32),
        jax.ShapeDtypeStruct((N_IDX_HEADS, m), F32),
        conv_shape,
    )
    out_specs = (
        pl.BlockSpec((tm, D_CONV), row),
        pl.BlockSpec((tm, N_HEADS_B * HD_B), row),
        pl.BlockSpec((tm, N_KV_B * HD_B), row),
        pl.BlockSpec((tm, N_KV_B * HD_B), row),
        pl.BlockSpec((tm, N_IDX_HEADS * IDX_DIM), row),
        pl.BlockSpec((tm, IDX_DIM), row),
        pl.BlockSpec((N_IDX_HEADS, tm), lambda i: (0, i)),
        conv_spec,
    )
    in_specs = [
        pl.BlockSpec((tm, D_MODEL), row),
        pl.BlockSpec((1, D_MODEL), const),
        pl.BlockSpec((D_MODEL, _EV_END), const),
        pl.BlockSpec((N_IDX_HEADS, D_MODEL), const),
        pl.BlockSpec((CONV_W, D_CONV), const),
        tab_spec, tab_spec, tab_spec,
        init_spec,
    ]
    return pl.pallas_call(
        functools.partial(_even_proj_body, tiles_per_seq=tiles_per_seq, time_major_b=time_major_b),
        grid=(nt,), in_specs=in_specs, out_specs=out_specs, out_shape=out_shape,
        scratch_shapes=scratch, compiler_params=_cparams(("arbitrary",)),
        name="even_proj",
    )(x, g, w, wwi, cw, *tabs, init)


def _mix_out_body(*refs, n_in):
    ins = refs[:n_in]
    ws = refs[n_in:2 * n_in]
    x_ref, g1_ref, g2_ref, wq_ref, x1_ref, qx_ref = refs[2 * n_in:]
    y = _dot(ins[0][...], ws[0][...])
    for a, w in zip(ins[1:], ws[1:]):
        y = y + _dot(a[...], w[...])
    x1 = x_ref[...] + _rms(y, g1_ref[...])
    x1_ref[...] = x1
    xn = _rms(x1, g2_ref[...]).astype(BF16)
    qx_ref[...] = (_dot(xn, wq_ref[...]) * HD_X ** -0.5).astype(BF16)


def _mix_out(ins, ws, x, g1, g2, wq, *, tm):
    m = x.shape[0]
    row = lambda i: (i, 0)
    const = lambda i: (0, 0)
    n_in = len(ins)
    in_specs = ([pl.BlockSpec((tm, a.shape[1]), row) for a in ins]
                + [pl.BlockSpec(w.shape, const) for w in ws]
                + [pl.BlockSpec((tm, D_MODEL), row), pl.BlockSpec((1, D_MODEL), const),
                   pl.BlockSpec((1, D_MODEL), const), pl.BlockSpec(wq.shape, const)])
    nq = wq.shape[1]
    return pl.pallas_call(
        functools.partial(_mix_out_body, n_in=n_in),
        grid=(m // tm,), in_specs=in_specs,
        out_specs=(pl.BlockSpec((tm, D_MODEL), row), pl.BlockSpec((tm, nq), row)),
        out_shape=(jax.ShapeDtypeStruct((m, D_MODEL), F32), jax.ShapeDtypeStruct((m, nq), BF16)),
        compiler_params=_cparams(("parallel",)), name="mix_out",
    )(*ins, *ws, x, g1, g2, wq)


def _mem_kv_body(mem_ref, g_ref, wk_ref, wv_ref, mk_ref, mv_ref):
    mn = _rms(mem_ref[...], g_ref[0]).astype(BF16)
    mk_ref[0] = _dot(mn, wk_ref[0])
    mv_ref[0] = _dot(mn, wv_ref[0])


def _mem_kv(mem, g, wk, wv, *, tm):
    m = mem.shape[0]
    depth = g.shape[0]
    n = wk.shape[2]
    wspec = pl.BlockSpec((1, D_MODEL, n), lambda l, i: (l, 0, 0))
    ospec = pl.BlockSpec((1, tm, n), lambda l, i: (l, i, 0))
    oshape = jax.ShapeDtypeStruct((depth, m, n), F32)
    return pl.pallas_call(
        _mem_kv_body, grid=(depth, m // tm),
        in_specs=[pl.BlockSpec((tm, D_MODEL), lambda l, i: (i, 0)),
                  pl.BlockSpec((1, 1, D_MODEL), lambda l, i: (l, 0, 0)), wspec, wspec],
        out_specs=(ospec, ospec), out_shape=(oshape, oshape),
        compiler_params=_cparams(("parallel", "parallel")), name="mem_kv",
    )(mem, g, wk, wv)


def _xattn_body(q_ref, mk_ref, mv_ref, x_ref, wo_ref, g_ref, o_ref):
    q = q_ref[0]
    mk = mk_ref[0].astype(BF16)
    mv = mv_ref[0].astype(BF16)
    outs = []
    for h in range(N_HEADS_X):
        sl = slice(h * HD_X, (h + 1) * HD_X)
        s = _dot_nt(q[:, sl], mk[:, sl])
        s = s - jnp.max(s, axis=-1, keepdims=True)
        p = jnp.exp(s)
        p = p / jnp.sum(p, axis=-1, keepdims=True)
        outs.append(_dot(p.astype(BF16), mv[:, sl]))
    o = jnp.concatenate(outs, axis=-1).astype(BF16)
    o_ref[0] = x_ref[0] + _rms(_dot(o, wo_ref[...]), g_ref[...])


def _xattn(q, mk, mv, x, wo, g, *, tm):
    b, t, _ = x.shape
    nq = q.shape[2]
    tile = lambda i, j: (i, j, 0)
    per_b = lambda i, j: (i, 0, 0)
    const = lambda i, j: (0, 0)
    return pl.pallas_call(
        _xattn_body, grid=(b, t // tm),
        in_specs=[pl.BlockSpec((1, tm, nq), tile),
                  pl.BlockSpec((1, N_MEM, nq), per_b), pl.BlockSpec((1, N_MEM, nq), per_b),
                  pl.BlockSpec((1, tm, D_MODEL), tile),
                  pl.BlockSpec(wo.shape, const), pl.BlockSpec((1, D_MODEL), const)],
        out_specs=pl.BlockSpec((1, tm, D_MODEL), tile),
        out_shape=jax.ShapeDtypeStruct((b, t, D_MODEL), F32),
        compiler_params=_cparams(("parallel", "parallel")), name="xattn",
    )(q, mk, mv, x, wo, g)


def _mlp_body(x_ref, g4_ref, g5_ref, w1_ref, w2_ref, o_ref, xn_ref, acc_ref):
    k = pl.program_id(1)

    @pl.when(k == 0)
    def _():
        xn_ref[...] = _rms(x_ref[...], g4_ref[...]).astype(BF16)

    h = jnp.maximum(_dot(xn_ref[...], w1_ref[...]), 0.0)
    part = _dot((h * h).astype(BF16), w2_ref[...])

    @pl.when(k == 0)
    def _():
        acc_ref[...] = part

    @pl.when(k > 0)
    def _():
        acc_ref[...] += part

    @pl.when(k == pl.num_programs(1) - 1)
    def _():
        o_ref[...] = x_ref[...] + _rms(acc_ref[...], g5_ref[...])


def _mlp(x, g4, g5, w1, w2, *, tm, tf):
    m = x.shape[0]
    f = w1.shape[1]
    row = lambda i, k: (i, 0)
    const = lambda i, k: (0, 0)
    return pl.pallas_call(
        _mlp_body, grid=(m // tm, f // tf),
        in_specs=[pl.BlockSpec((tm, D_MODEL), row), pl.BlockSpec((1, D_MODEL), const),
                  pl.BlockSpec((1, D_MODEL), const),
                  pl.BlockSpec((D_MODEL, tf), lambda i, k: (0, k)),
                  pl.BlockSpec((tf, D_MODEL), lambda i, k: (k, 0))],
        out_specs=pl.BlockSpec((tm, D_MODEL), row),
        out_shape=jax.ShapeDtypeStruct((m, D_MODEL), F32),
        scratch_shapes=[pltpu.VMEM((tm, D_MODEL), BF16), pltpu.VMEM((tm, D_MODEL), F32)],
        compiler_params=_cparams(("parallel", "arbitrary")), name="mlp",
    )(x, g4, g5, w1, w2)


def _sortable_key(score):
    bits = pltpu.bitcast(score, I32)
    return bits ^ ((bits >> 31) & 0x7FFFFFFF)


def _dsa_prompt_body(qit_ref, wit_ref, ki_ref, qt_ref, k_ref, vt_ref, ot_ref,
                     key_ref, m_ref, l_ref, acc_ref, *, ktop, idx_bits):
    tq = qit_ref.shape[2]
    kc = tq
    i = pl.program_id(1)
    nck = i + 1
    kiota = lax.broadcasted_iota(I32, (kc, tq), 0)
    qidx = i * tq + lax.broadcasted_iota(I32, (kc, tq), 1)

    def chunk_off(c):
        return pl.multiple_of(c * kc, kc)

    def score_chunk(c, carry):
        off = chunk_off(c)
        kic = ki_ref[0, pl.ds(off, kc), :]
        sc = jnp.zeros((kc, tq), F32)
        for h in range(N_IDX_HEADS):
            s = _dot(kic, qit_ref[0, h * IDX_DIM:(h + 1) * IDX_DIM, :])
            sc = sc + wit_ref[h:h + 1, :] * jnp.maximum(s, 0.0)
        sc = jnp.where(kiota + off <= qidx, sc, -jnp.inf)
        key_ref[pl.ds(off, kc), :] = _sortable_key(sc)
        return carry

    lax.fori_loop(0, nck, score_chunk, 0)

    def count(pred):
        def body(c, acc):
            off = chunk_off(c)
            hit = pred(key_ref[pl.ds(off, kc), :], kiota + off).astype(I32)
            return acc + jnp.sum(hit.reshape(kc // SUBLANES, SUBLANES, tq), axis=0)

        acc = lax.fori_loop(0, nck, body, jnp.zeros((SUBLANES, tq), I32))
        return jnp.sum(acc, axis=0, keepdims=True)

    def bit_step(j, lo):
        cand = lo + jnp.left_shift(jnp.int32(1), 31 - j)
        cnt = count(lambda blk, _: blk >= cand)
        return jnp.where(cnt >= ktop, cand, lo)

    thr = lax.fori_loop(0, 32, bit_step, jnp.full((1, tq), INT_MIN, I32))

    cnt_gt = count(lambda blk, _: blk > thr)
    cnt_ge = count(lambda blk, _: blk >= thr)
    tie = jnp.logical_and(thr > KEY_NEG_INF, cnt_ge > ktop)

    @pl.when(jnp.max(tie.astype(I32)) > 0)
    def _():
        need = ktop - cnt_gt

        def idx_step(j, res):
            cand = res + jnp.left_shift(jnp.int32(1), idx_bits - 1 - j)
            cnt = count(lambda blk, kidx: jnp.logical_and(blk == thr, kidx < cand))
            return jnp.where(cnt < need, cand, res)

        jcut = lax.fori_loop(0, idx_bits, idx_step, jnp.zeros((1, tq), I32))

        def drop_chunk(c, carry):
            off = chunk_off(c)
            blk = key_ref[pl.ds(off, kc), :]
            drop = jnp.logical_and(blk == thr, kiota + off > jcut)
            key_ref[pl.ds(off, kc), :] = jnp.where(drop, blk - 1, blk)
            return carry

        lax.fori_loop(0, nck, drop_chunk, 0)

    thr_eff = jnp.maximum(thr, KEY_NEG_INF + 1)
    m_ref[...] = jnp.full(m_ref.shape, NEG_BIG, F32)
    l_ref[...] = jnp.zeros(l_ref.shape, F32)
    acc_ref[...] = jnp.zeros(acc_ref.shape, F32)

    def attn_chunk(c, carry):
        off = chunk_off(c)
        mask = key_ref[pl.ds(off, kc), :] >= thr_eff
        for g in range(N_KV_B):
            kg = k_ref[0, g, pl.ds(off, kc), :]
            vtg = vt_ref[0, g, c]
            for hh in range(N_HEADS_B // N_KV_B):
                h = g * (N_HEADS_B // N_KV_B) + hh
                s = _dot(kg, qt_ref[0, h * HD_B:(h + 1) * HD_B, :])
                s = jnp.where(mask, s, NEG_BIG)
                m_old = m_ref[h:h + 1, :]
                m_new = jnp.maximum(m_old, jnp.max(s, axis=0, keepdims=True))
                alpha = jnp.exp(m_old - m_new)
                p = jnp.exp(s - m_new)
                l_ref[h:h + 1, :] = alpha * l_ref[h:h + 1, :] + jnp.sum(p, axis=0, keepdims=True)
                acc_ref[h] = alpha * acc_ref[h] + _dot(vtg, p.astype(BF16))
                m_ref[h:h + 1, :] = m_new
        return carry

    lax.fori_loop(0, nck, attn_chunk, 0)
    for h in range(N_HEADS_B):
        ot_ref[0, h * HD_B:(h + 1) * HD_B, :] = (acc_ref[h] / l_ref[h:h + 1, :]).astype(BF16)


def _dsa_prompt(qit, wit, ki, qt, k, vt, *, tq):
    b, hd, t = qt.shape
    nt = t // tq
    ktop = min(TOPK_MAX, t // 4)
    idx_bits = max(1, (t - 1).bit_length())
    return pl.pallas_call(
        functools.partial(_dsa_prompt_body, ktop=ktop, idx_bits=idx_bits),
        grid=(b, nt),
        in_specs=[
            pl.BlockSpec((1, hd, tq), lambda bi, i: (bi, 0, i)),
            pl.BlockSpec((N_IDX_HEADS, tq), lambda bi, i: (0, bi * nt + i)),
            pl.BlockSpec((1, t, IDX_DIM), lambda bi, i: (bi, 0, 0)),
            pl.BlockSpec((1, hd, tq), lambda bi, i: (bi, 0, i)),
            pl.BlockSpec((1, N_KV_B, t, HD_B), lambda bi, i: (bi, 0, 0, 0)),
            pl.BlockSpec((1, N_KV_B, nt, HD_B, tq), lambda bi, i: (bi, 0, 0, 0, 0)),
        ],
        out_specs=pl.BlockSpec((1, hd, tq), lambda bi, i: (bi, 0, i)),
        out_shape=jax.ShapeDtypeStruct((b, hd, t), BF16),
        scratch_shapes=[pltpu.VMEM((t, tq), I32), pltpu.VMEM((N_HEADS_B, tq), F32),
                        pltpu.VMEM((N_HEADS_B, tq), F32), pltpu.VMEM((N_HEADS_B, HD_B, tq), F32)],
        compiler_params=_cparams(("parallel", "arbitrary")), name="dsa_prompt",
    )(qit, wit, ki, qt, k, vt)


_PG = 4
_CK = _PG * PAGE_SIZE
_TP = SUBLANES


def _dsa_sample_body(pt_ref, qi_ref, w_ref, kin_ref, q_ref, kn_ref, vn_ref,
                     ci_hbm, ck_hbm, cv_hbm, o_ref,
                     ibuf, kbuf, vbuf, key_ref, sem, *, n_pages, t_new, ktop, idx_bits):
    b = pl.program_id(0)
    nch = n_pages // _PG
    past = n_pages * PAGE_SIZE

    streams = ((ci_hbm, ibuf), (ck_hbm, kbuf), (cv_hbm, vbuf))

    def page_copy(which, p, pg):
        hbm, buf = streams[which]
        return pltpu.make_async_copy(hbm.at[pg], buf.at[p], sem.at[which])

    def issue(p, carry):
        pg = pt_ref[b, p]
        for which in range(len(streams)):
            page_copy(which, p, pg).start()
        return carry

    lax.fori_loop(0, n_pages, issue, 0)

    def wait_all(which):
        def w(p, carry):
            page_copy(which, p, 0).wait()
            return carry

        lax.fori_loop(0, n_pages, w, 0)

    lane = lax.broadcasted_iota(I32, (_TP, _CK), 1)
    rowi = lax.broadcasted_iota(I32, (_TP, _CK), 0)

    qi = qi_ref[0]
    wq = w_ref[0]
    nt = qi.shape[0] // N_IDX_HEADS

    def scores(kp):
        s = jnp.maximum(_dot_nt(qi, kp), 0.0) * wq
        return jnp.sum(s.reshape(nt, N_IDX_HEADS, _CK), axis=1)

    wait_all(0)

    def sc_chunk(c, carry):
        kp = ibuf[pl.ds(c * _PG, _PG)].reshape(_CK, IDX_DIM).astype(BF16)
        key_ref[c] = _sortable_key(scores(kp))
        return carry

    lax.fori_loop(0, nch, sc_chunk, 0)
    sc_new = scores(kin_ref[0])
    admissible = jnp.logical_and(lane <= rowi, rowi < t_new)
    key_ref[nch] = _sortable_key(jnp.where(admissible, sc_new, -jnp.inf))

    def count(pred):
        def body(c, acc):
            return acc + pred(key_ref[c], lane + c * _CK).astype(I32)

        acc = lax.fori_loop(0, nch + 1, body, jnp.zeros((_TP, _CK), I32))
        return jnp.sum(acc, axis=1, keepdims=True)

    def bit_step(j, lo):
        cand = lo + jnp.left_shift(jnp.int32(1), 31 - j)
        cnt = count(lambda blk, _: blk >= cand)
        return jnp.where(cnt >= ktop, cand, lo)

    thr = lax.fori_loop(0, 32, bit_step, jnp.full((_TP, 1), INT_MIN, I32))
    cnt_gt = count(lambda blk, _: blk > thr)
    cnt_ge = count(lambda blk, _: blk >= thr)
    tie = jnp.logical_and(thr > KEY_NEG_INF, cnt_ge > ktop)

    @pl.when(jnp.max(tie.astype(I32)) > 0)
    def _():
        need = ktop - cnt_gt

        def idx_step(j, res):
            cand = res + jnp.left_shift(jnp.int32(1), idx_bits - 1 - j)
            cnt = count(lambda blk, kidx: jnp.logical_and(blk == thr, kidx < cand))
            return jnp.where(cnt < need, cand, res)

        jcut = lax.fori_loop(0, idx_bits, idx_step, jnp.zeros((_TP, 1), I32))

        def drop_chunk(c, carry):
            blk = key_ref[c]
            drop = jnp.logical_and(blk == thr, lane + c * _CK > jcut)
            key_ref[c] = jnp.where(drop, blk - 1, blk)
            return carry

        lax.fori_loop(0, nch + 1, drop_chunk, 0)

    thr_eff = jnp.maximum(thr, KEY_NEG_INF + 1)
    qp = q_ref[0]
    reps = qp.shape[0] // _TP

    def attend(c, kc, vc, carry):
        m, l, acc = carry
        bias = jnp.where(key_ref[c] >= thr_eff, 0.0, NEG_BIG)
        s = _dot_nt(qp, kc) + jnp.concatenate([bias] * reps, axis=0)
        m_new = jnp.maximum(m, jnp.max(s, axis=-1, keepdims=True))
        alpha = jnp.exp(m - m_new)
        p = jnp.exp(s - m_new)
        l = alpha * l + jnp.sum(p, axis=-1, keepdims=True)
        acc = alpha * acc + _dot(p.astype(BF16), vc)
        return m_new, l, acc

    wait_all(1)
    wait_all(2)

    def at_chunk(c, carry):
        kc = kbuf[pl.ds(c * _PG, _PG)].reshape(_CK, N_KV_B * HD_B).astype(BF16)
        vc = vbuf[pl.ds(c * _PG, _PG)].reshape(_CK, N_KV_B * HD_B).astype(BF16)
        return attend(c, kc, vc, carry)

    nr = qp.shape[0]
    init = (jnp.full((nr, 1), NEG_BIG, F32), jnp.zeros((nr, 1), F32),
            jnp.zeros((nr, N_KV_B * HD_B), F32))
    carry = lax.fori_loop(0, nch, at_chunk, init)
    m, l, acc = attend(nch, kn_ref[0], vn_ref[0], carry)
    o_ref[0] = acc / l


def _dsa_sample(page_table, qi, w, kin, qp, kn, vn, cache_i, cache_k, cache_v, *, t_new):
    b, n_pages = page_table.shape
    assert n_pages % _PG == 0
    nr = qp.shape[1]
    total = n_pages * PAGE_SIZE + t_new
    ktop = min(TOPK_MAX, total // 4)
    idx_bits = max(1, (total - 1).bit_length())
    nch = n_pages // _PG
    per_b = lambda i, pt: (i, 0, 0)
    grid_spec = pltpu.PrefetchScalarGridSpec(
        num_scalar_prefetch=1, grid=(b,),
        in_specs=[pl.BlockSpec((1,) + qi.shape[1:], per_b), pl.BlockSpec((1,) + w.shape[1:], per_b),
                  pl.BlockSpec((1,) + kin.shape[1:], per_b), pl.BlockSpec((1,) + qp.shape[1:], per_b),
                  pl.BlockSpec((1,) + kn.shape[1:], per_b), pl.BlockSpec((1,) + vn.shape[1:], per_b),
                  pl.BlockSpec(memory_space=pl.ANY), pl.BlockSpec(memory_space=pl.ANY),
                  pl.BlockSpec(memory_space=pl.ANY)],
        out_specs=pl.BlockSpec((1, nr, N_KV_B * HD_B), per_b),
        scratch_shapes=[pltpu.VMEM((n_pages, PAGE_SIZE, IDX_DIM), F32),
                        pltpu.VMEM((n_pages, PAGE_SIZE, N_KV_B * HD_B), F32),
                        pltpu.VMEM((n_pages, PAGE_SIZE, N_KV_B * HD_B), F32),
                        pltpu.VMEM((nch + 1, _TP, _CK), I32),
                        pltpu.SemaphoreType.DMA((3,))])
    return pl.pallas_call(
        functools.partial(_dsa_sample_body, n_pages=n_pages, t_new=t_new, ktop=ktop, idx_bits=idx_bits),
        grid_spec=grid_spec,
        out_shape=jax.ShapeDtypeStruct((b, nr, N_KV_B * HD_B), F32),
        compiler_params=_cparams(("arbitrary",)), name="dsa_sample",
    )(page_table, qi, w, kin, qp, kn, vn, cache_i, cache_k, cache_v)


def _log_sigmoid(x):
    return jnp.minimum(x, 0.0) - jnp.log1p(jnp.exp(-jnp.abs(x)))


def _odd_proj_body(x_ref, g_ref, w_ref, wg_ref, wgt_ref, bg_ref, bgt_ref,
                   q_ref, k_ref, v_ref, so_ref, gate_ref, gatet_ref):
    xn = _rms(x_ref[...], g_ref[...]).astype(BF16)

    def seg(a, b):
        return _dot(xn, w_ref[:, a:b])

    q_ref[...] = (seg(_OD_Q, _OD_K) * DQK_C ** -0.5).astype(BF16)
    k_ref[...] = seg(_OD_K, _OD_V).astype(BF16)
    v_ref[...] = seg(_OD_V, _OD_O).astype(BF16)
    so_ref[...] = jax.nn.sigmoid(seg(_OD_O, _OD_END)).astype(BF16)
    nh = N_HEADS_C
    gate = _dot(xn, wg_ref[...]) + bg_ref[...]
    col = lax.broadcasted_iota(I32, gate.shape, 1)
    gate_ref[...] = jnp.where(col < nh, gate, _log_sigmoid(gate))
    gatet = _dot_nt(wgt_ref[...], xn) + bgt_ref[...]
    rowi = lax.broadcasted_iota(I32, gatet.shape, 0)
    gatet_ref[...] = jnp.where(rowi < nh, gatet, _log_sigmoid(gatet))


def _odd_proj(x, g, w, wg, wgt, bg, bgt, *, tm):
    m = x.shape[0]
    row = lambda i: (i, 0)
    const = lambda i: (0, 0)
    nqk = N_HEADS_C * DQK_C
    nv = N_HEADS_C * DV_C
    ng = 2 * N_HEADS_C
    return pl.pallas_call(
        _odd_proj_body, grid=(m // tm,),
        in_specs=[pl.BlockSpec((tm, D_MODEL), row), pl.BlockSpec((1, D_MODEL), const),
                  pl.BlockSpec((D_MODEL, _OD_END), const), pl.BlockSpec((D_MODEL, ng), const),
                  pl.BlockSpec((ng, D_MODEL), const), pl.BlockSpec((1, ng), const),
                  pl.BlockSpec((ng, 1), const)],
        out_specs=(pl.BlockSpec((tm, nqk), row), pl.BlockSpec((tm, nqk), row),
                   pl.BlockSpec((tm, nv), row), pl.BlockSpec((tm, nv), row),
                   pl.BlockSpec((tm, ng), row), pl.BlockSpec((ng, tm), lambda i: (0, i))),
        out_shape=(jax.ShapeDtypeStruct((m, nqk), BF16), jax.ShapeDtypeStruct((m, nqk), BF16),
                   jax.ShapeDtypeStruct((m, nv), BF16), jax.ShapeDtypeStruct((m, nv), BF16),
                   jax.ShapeDtypeStruct((m, ng), F32), jax.ShapeDtypeStruct((ng, m), F32)),
        compiler_params=_cparams(("parallel",)), name="odd_proj",
    )(x, g, w, wg, wgt, bg, bgt)


def _split3(x):
    hi = x.astype(BF16)
    r1 = x - hi.astype(F32)
    mid = r1.astype(BF16)
    lo = (r1 - mid.astype(F32)).astype(BF16)
    return hi, mid, lo


def _mlstm_body(q_ref, k_ref, kt_ref, v_ref, so_ref, gate_ref, gatet_ref, hg_ref,
                c0_ref, n0_ref, m0_ref,
                h_ref, ct_ref, n_ref, m_ref, cs_ref, ns_ref, ms_ref):
    lc = q_ref.shape[2]
    nh = N_HEADS_C
    c = pl.program_id(1)

    @pl.when(c == 0)
    def _():
        cs_ref[...] = c0_ref[0]
        ns_ref[...] = n0_ref[0]
        ms_ref[...] = jnp.broadcast_to(m0_ref[0], ms_ref.shape)

    ri = lax.broadcasted_iota(I32, (lc, lc), 0)
    ci = lax.broadcasted_iota(I32, (lc, lc), 1)
    causal = ci <= ri
    tril = causal.astype(BF16)
    triu = (ri <= ci).astype(BF16)
    gate = gate_ref[0]
    gatet = gatet_ref[0]
    lf3 = _split3(gate[:, nh:])
    bcols = _dot(tril, lf3[0]) + _dot(tril, lf3[1]) + _dot(tril, lf3[2])
    lft3 = _split3(gatet[nh:, :])
    brows = _dot(lft3[0], triu) + _dot(lft3[1], triu) + _dot(lft3[2], triu)
    ns = ns_ref[...]
    ms = ms_ref[...]
    ns_new, ms_new = [], []
    for h in range(nh):
        qh = q_ref[0, h]
        kh = k_ref[0, h]
        vh = v_ref[0, h]
        bc = bcols[:, h:h + 1]
        br = brows[h:h + 1, :]
        igc = gate[:, h:h + 1]
        igr = gatet[h:h + 1, :]
        m_prev = ms[h:h + 1, 0:1]
        a = bc + m_prev
        d = jnp.where(causal, bc + (igr - br), NEG_BIG)
        mj = jnp.maximum(a, jnp.max(d, axis=-1, keepdims=True))
        s = _dot_nt(qh, kh) * jnp.exp(d - mj)
        aw = jnp.exp(a - mj)
        ct = cs_ref[h]
        num = _dot(s.astype(BF16), vh) + aw * _dot(qh, ct.astype(BF16))
        n_h = ns[h:h + 1, :]
        qn = jnp.sum(qh.astype(F32) * n_h, axis=-1, keepdims=True)
        den = jnp.sum(s, axis=-1, keepdims=True) + aw * qn
        hh = num / jnp.maximum(jnp.abs(den), jnp.exp(-mj))
        hh = hh * lax.rsqrt(jnp.mean(hh * hh, axis=-1, keepdims=True) + NORM_EPS)
        sl = slice(h * DV_C, (h + 1) * DV_C)
        h_ref[0, :, sl] = (hh * hg_ref[:, sl] * so_ref[0, :, sl].astype(F32)).astype(BF16)
        b_last = br[:, lc - 1:lc]
        g_row = b_last - br + igr
        m_new = jnp.maximum(b_last + m_prev, jnp.max(g_row, axis=-1, keepdims=True))
        gw_col = jnp.exp(b_last - bc + igc - m_new)
        decay = jnp.exp(b_last + m_prev - m_new)
        cs_ref[h] = decay * ct + _dot(kt_ref[0, h], (vh.astype(F32) * gw_col).astype(BF16))
        ns_new.append(decay * n_h + jnp.sum(kh.astype(F32) * gw_col, axis=0, keepdims=True))
        ms_new.append(jnp.broadcast_to(m_new, (1, ms.shape[1])))
    ns_ref[...] = jnp.concatenate(ns_new, axis=0)
    ms_ref[...] = jnp.concatenate(ms_new, axis=0)

    @pl.when(c == pl.num_programs(1) - 1)
    def _():
        ct_ref[0] = cs_ref[...]
        n_ref[0] = ns_ref[...]
        m_ref[0] = ms_ref[...]


def _mlstm(q, k, kt, v, so, gate, gatet, hg, c0t, n0, m0, *, lc):
    b, nh, t, dqk = q.shape
    dv = v.shape[3]
    ng = gate.shape[2]
    per_b4 = lambda bi, c: (bi, 0, 0, 0)
    per_b3 = lambda bi, c: (bi, 0, 0)
    return pl.pallas_call(
        _mlstm_body, grid=(b, t // lc),
        in_specs=[pl.BlockSpec((1, nh, lc, dqk), lambda bi, c: (bi, 0, c, 0)),
                  pl.BlockSpec((1, nh, lc, dqk), lambda bi, c: (bi, 0, c, 0)),
                  pl.BlockSpec((1, nh, dqk, lc), lambda bi, c: (bi, 0, 0, c)),
                  pl.BlockSpec((1, nh, lc, dv), lambda bi, c: (bi, 0, c, 0)),
                  pl.BlockSpec((1, lc, nh * dv), lambda bi, c: (bi, c, 0)),
                  pl.BlockSpec((1, lc, ng), lambda bi, c: (bi, c, 0)),
                  pl.BlockSpec((1, ng, lc), lambda bi, c: (bi, 0, c)),
                  pl.BlockSpec((1, nh * dv), lambda bi, c: (0, 0)),
                  pl.BlockSpec((1, nh, dqk, dv), per_b4),
                  pl.BlockSpec((1, nh, dqk), per_b3),
                  pl.BlockSpec((1, nh, 1), per_b3)],
        out_specs=(pl.BlockSpec((1, lc, nh * dv), lambda bi, c: (bi, c, 0)),
                   pl.BlockSpec((1, nh, dqk, dv), per_b4),
                   pl.BlockSpec((1, nh, dqk), per_b3),
                   pl.BlockSpec((1, nh, LANES), per_b3)),
        out_shape=(jax.ShapeDtypeStruct((b, t, nh * dv), BF16),
                   jax.ShapeDtypeStruct((b, nh, dqk, dv), F32),
                   jax.ShapeDtypeStruct((b, nh, dqk), F32),
                   jax.ShapeDtypeStruct((b, nh, LANES), F32)),
        scratch_shapes=[pltpu.VMEM((nh, dqk, dv), F32), pltpu.VMEM((nh, dqk), F32),
                        pltpu.VMEM((nh, LANES), F32)],
        compiler_params=_cparams(("parallel", "arbitrary")), name="mlstm",
    )(q, k, kt, v, so, gate, gatet, hg, c0t, n0, m0)


def _prep_even(w_in, w_out):
    sizes = [D_CONV, D_CONV, D_CONV, N_HEADS_B * HD_B, N_KV_B * HD_B, N_KV_B * HD_B,
             N_IDX_HEADS * IDX_DIM, N_IDX_HEADS]
    offs = [sum(sizes[:j + 1]) for j in range(len(sizes))]
    u, gb, gc, q, k, v, qi, wi, ki = jnp.split(w_in, offs, axis=1)
    pad = jnp.zeros((D_MODEL, LANES - IDX_DIM), w_in.dtype)
    w = jnp.concatenate([u, gb, gc, q, k, v, qi, ki, pad], axis=1).astype(BF16)
    return w, wi.T.astype(BF16), w_out[:D_CONV].astype(BF16), w_out[D_CONV:].astype(BF16)


def _prep_odd(w_in, b_i, b_f):
    nmain = _OD_END
    w = w_in[:, :nmain].astype(BF16)
    wg = w_in[:, nmain:].astype(BF16)
    bg = jnp.concatenate([b_i, b_f])
    return w, wg, wg.T, bg[None, :], bg[:, None]


def _tile(n, pref):
    return pref if n % pref == 0 else n


def _pad_axis(a, axis, n):
    if a.shape[axis] == n:
        return a
    widths = [(0, 0)] * a.ndim
    widths[axis] = (0, n - a.shape[axis])
    return jnp.pad(a, widths)


def _tail(x2, qx, mk, mv, wo, g3, g4, g5, w1, w2, *, b, t):
    t_pad = max(t, 2 * SUBLANES)
    tx = _tile(t_pad, 512)
    q3 = _pad_axis(qx.reshape(b, t, -1), 1, t_pad)
    x3 = _pad_axis(x2.reshape(b, t, D_MODEL), 1, t_pad)
    x3 = _xattn(q3, mk, mv, x3, wo, g3, tm=tx)[:, :t].reshape(b * t, D_MODEL)
    return _mlp(x3, g4, g5, w1, w2, tm=_tile(b * t, 1024), tf=1024)


def _even_prompt(x2, g, we, cw, wxq, *, b, t, tm):
    w, wwi, wo_a, wo_b = we
    tq = _tile(t, 256)
    tabs = _rope_tables(jnp.arange(t))
    init = jnp.zeros((b, CONV_W - 1, D_CONV), F32)
    ya, q, k, v, qi, ki, wit, conv = _even_proj(x2, g[0:1], w, wwi, cw, tabs, init, tm=tm, seq_len=t)
    qt = q.reshape(b, t, -1).transpose(0, 2, 1)
    qit = qi.reshape(b, t, -1).transpose(0, 2, 1)
    kib = ki.reshape(b, t, IDX_DIM).astype(BF16)
    kh = k.reshape(b, t, N_KV_B, HD_B).transpose(0, 2, 1, 3).astype(BF16)
    vt = v.reshape(b, t // tq, tq, N_KV_B, HD_B).transpose(0, 3, 1, 4, 2).astype(BF16)
    ybt = _dsa_prompt(qit, wit, kib, qt, kh, vt, tq=tq)
    yb = ybt.transpose(0, 2, 1).reshape(b * t, -1)
    x1, qx = _mix_out([ya, yb], [wo_a, wo_b], x2, g[1:2], g[2:3], wxq, tm=tm)
    state = (k.reshape(b, t, N_KV_B, HD_B), v.reshape(b, t, N_KV_B, HD_B),
             ki.reshape(b, t, IDX_DIM), conv)
    return x1, qx, state


def _odd_mixer(x2, g, wod, hg, w_out, wxq, c0, n0, m0, *, b, t, tm, lc):
    w, wg, wgt, bg, bgt = wod
    q, k, v, so, gate, gatet = _odd_proj(x2, g[0:1], w, wg, wgt, bg, bgt, tm=tm)
    nh = N_HEADS_C
    tp = -(-t // lc) * lc
    pad_t = lambda a: _pad_axis(a, 1, tp)
    qh = pad_t(q.reshape(b, t, nh, DQK_C)).transpose(0, 2, 1, 3)
    k4 = pad_t(k.reshape(b, t, nh, DQK_C))
    kh = k4.transpose(0, 2, 1, 3)
    kt = k4.transpose(0, 2, 3, 1)
    vh = pad_t(v.reshape(b, t, nh, DV_C)).transpose(0, 2, 1, 3)
    so3 = pad_t(so.reshape(b, t, nh * DV_C))
    gate3 = gate.reshape(b, t, 2 * nh)
    gatet3 = gatet.reshape(2 * nh, b, t).transpose(1, 0, 2)
    if tp > t:
        neutral = jnp.concatenate([jnp.full((nh,), NEG_BIG, F32), jnp.zeros((nh,), F32)])
        gate3 = jnp.concatenate(
            [gate3, jnp.broadcast_to(neutral[None, None, :], (b, tp - t, 2 * nh))], axis=1)
        gatet3 = jnp.concatenate(
            [gatet3, jnp.broadcast_to(neutral[None, :, None], (b, 2 * nh, tp - t))], axis=2)
    h, ct, n, m = _mlstm(qh, kh, kt, vh, so3, gate3, gatet3, hg,
                         c0.transpose(0, 1, 3, 2), n0, m0[:, :, None], lc=lc)
    h2 = h[:, :t].reshape(b * t, nh * DV_C)
    x1, qx = _mix_out([h2], [w_out], x2, g[1:2], g[2:3], wxq, tm=tm)
    return x1, qx, (ct.transpose(0, 1, 3, 2), n, m[:, :, 0])


def _even_sample(x2, g, we, cw, wxq, cache_i, cache_k, cache_v, state_conv, page_table, *, b, t):
    w, wwi, wo_a, wo_b = we
    n_pages = page_table.shape[1]
    pos = n_pages * PAGE_SIZE + jnp.arange(t)
    tabs = tuple(jnp.repeat(tab, b, axis=0) for tab in _rope_tables(pos))
    to_tm = lambda a: a.reshape(b, t, -1).transpose(1, 0, 2).reshape(t * b, -1)
    to_bm = lambda a: a.reshape(t, b, -1).transpose(1, 0, 2)
    init = state_conv.transpose(1, 0, 2).reshape(2 * b, D_CONV)
    ya, q, k, v, qi, ki, wit, conv = _even_proj(
        to_tm(x2), g[0:1], w, wwi, cw, tabs, init, tm=t * b, seq_len=t, time_major_b=b)
    ng = N_HEADS_B // N_KV_B
    qi_b = _pad_axis(to_bm(qi), 1, _TP).reshape(b, _TP * N_IDX_HEADS, IDX_DIM)
    w_b = _pad_axis(wit.reshape(N_IDX_HEADS, t, b).transpose(2, 1, 0), 1, _TP)
    w_b = w_b.reshape(b, _TP * N_IDX_HEADS, 1)
    kin = _pad_axis(to_bm(ki), 1, _CK).astype(BF16)
    kn = _pad_axis(to_bm(k), 1, _CK).astype(BF16)
    vn = _pad_axis(to_bm(v), 1, _CK).astype(BF16)
    q5 = _pad_axis(to_bm(q).reshape(b, t, N_KV_B, ng, HD_B).transpose(0, 2, 3, 1, 4), 3, _TP)
    zq = jnp.zeros_like(q5[:, 0])
    qp = jnp.stack([jnp.concatenate([q5[:, 0], zq], axis=-1),
                    jnp.concatenate([zq, q5[:, 1]], axis=-1)], axis=1)
    qp = qp.reshape(b, N_KV_B * ng * _TP, N_KV_B * HD_B)
    n_phys = cache_k.shape[0]
    o = _dsa_sample(page_table, qi_b, w_b, kin, qp, kn, vn,
                    cache_i, cache_k.reshape(n_phys, PAGE_SIZE, -1),
                    cache_v.reshape(n_phys, PAGE_SIZE, -1), t_new=t)
    o = o.reshape(b, N_KV_B, ng, _TP, N_KV_B, HD_B)[:, :, :, :t]
    yb = jnp.stack([o[:, 0, :, :, 0], o[:, 1, :, :, 1]], axis=1)
    yb = yb.transpose(3, 0, 1, 2, 4).reshape(t * b, N_HEADS_B * HD_B).astype(BF16)
    x1, qx = _mix_out([ya, yb], [wo_a, wo_b], to_tm(x2), g[1:2], g[2:3], wxq, tm=t * b)
    back = lambda a: to_bm(a).reshape(b * t, -1)
    state = (to_bm(k).reshape(b, t, N_KV_B, HD_B), to_bm(v).reshape(b, t, N_KV_B, HD_B),
             to_bm(ki), conv.reshape(2, b, D_CONV).transpose(1, 0, 2))
    return back(x1), back(qx), state


def kernel(x_prompt, x_sample, cache_k, cache_v, cache_idx_k, cache_mem_k, cache_mem_v, state_conv, state_C, state_n, state_m, page_table, mem_prompt, norm_g, w_in_even, conv_w, w_out_even, w_in_odd, b_i, b_f, hnorm_g, w_out_odd, mem_norm_g, w_xq, w_xk, w_xv, w_xo, w_ff1, w_ff2):
    bp, tp, _ = x_prompt.shape
    bs, ts, _ = x_sample.shape
    depth = norm_g.shape[0]
    tm_p = _tile(tp, 512)
    lc = _tile(tp, 256)
    nmem = mem_prompt.shape[1]
    mk_all, mv_all = _mem_kv(mem_prompt.reshape(bp * nmem, D_MODEL), mem_norm_g[:, None, :],
                             w_xk.astype(BF16), w_xv.astype(BF16), tm=_tile(bp * nmem, 512))
    xp = x_prompt.reshape(bp * tp, D_MODEL)
    xs = x_sample.reshape(bs * ts, D_MODEL)
    ev_p, ev_s, od_p, od_s = [], [], [], []
    for l in range(depth):
        g = norm_g[l]
        wxq = w_xq[l].astype(BF16)
        if l % 2 == 0:
            e = l // 2
            we = _prep_even(w_in_even[e], w_out_even[e])
            xp, qxp, st = _even_prompt(xp, g, we, conv_w[e], wxq, b=bp, t=tp, tm=tm_p)
            ev_p.append(st)
            xs, qxs, st = _even_sample(xs, g, we, conv_w[e], wxq, cache_idx_k[e], cache_k[e],
                                       cache_v[e], state_conv[e], page_table, b=bs, t=ts)
            ev_s.append(st)
        else:
            o = l // 2
            wod = _prep_odd(w_in_odd[o], b_i[o], b_f[o])
            hg = hnorm_g[o][None, :]
            wout = w_out_odd[o].astype(BF16)
            zc = jnp.zeros((bp, N_HEADS_C, DV_C, DQK_C), F32)
            zn = jnp.zeros((bp, N_HEADS_C, DQK_C), F32)
            zm = jnp.zeros((bp, N_HEADS_C), F32)
            xp, qxp, st = _odd_mixer(xp, g, wod, hg, wout, wxq, zc, zn, zm, b=bp, t=tp, tm=tm_p, lc=lc)
            od_p.append(st)
            xs, qxs, st = _odd_mixer(xs, g, wod, hg, wout, wxq, state_C[o], state_n[o], state_m[o],
                                     b=bs, t=ts, tm=bs * ts, lc=LANES)
            od_s.append(st)
        wxo = w_xo[l].astype(BF16)
        w1 = w_ff1[l].astype(BF16)
        w2 = w_ff2[l].astype(BF16)
        nx = N_HEADS_X * HD_X
        xp = _tail(xp, qxp, mk_all[l].reshape(bp, nmem, nx), mv_all[l].reshape(bp, nmem, nx),
                   wxo, g[3:4], g[4:5], g[5:6], w1, w2, b=bp, t=tp)
        xs = _tail(xs, qxs, cache_mem_k[l].reshape(bs, nmem, nx), cache_mem_v[l].reshape(bs, nmem, nx),
                   wxo, g[3:4], g[4:5], g[5:6], w1, w2, b=bs, t=ts)
    stack = lambda sts, j: jnp.stack([s[j] for s in sts])
    mem_shape = (depth, bp, nmem, N_HEADS_X, HD_X)
    return (xp.reshape(bp, tp, D_MODEL), xs.reshape(bs, ts, D_MODEL),
            stack(ev_p, 0), stack(ev_p, 1), stack(ev_p, 2), stack(ev_p, 3),
            stack(od_p, 0), stack(od_p, 1), stack(od_p, 2),
            mk_all.reshape(mem_shape), mv_all.reshape(mem_shape),
            stack(ev_s, 0), stack(ev_s, 1), stack(ev_s, 2), stack(ev_s, 3),
            stack(od_s, 0), stack(od_s, 1), stack(od_s, 2))
```

```python
import functools
import math

import jax
import jax.numpy as jnp
from jax import lax
from jax.experimental import pallas as pl
from jax.experimental.pallas import tpu as pltpu

F32 = jnp.float32
BF16 = jnp.bfloat16
I32 = jnp.int32

D_MODEL = 1024
D_CONV = 512
CONV_W = 3
N_HEADS_B = 8
N_KV_B = 2
HD_B = 64
N_IDX_HEADS = 8
IDX_DIM = 64
TOPK_MAX = 256
N_HEADS_C = 8
DQK_C = 64
DV_C = 128
N_MEM = 256
N_HEADS_X = 4
HD_X = 128
D_FF = 4096
PAGE_SIZE = 128
ROPE_THETA = 500000.0
NORM_EPS = 1e-6

LANES = 128
SUBLANES = 8
VMEM_LIMIT = 48 * 1024 * 1024

NEG_BIG = -1e30
INT_MIN = -(2 ** 31)
KEY_NEG_INF = -(2 ** 31) + 0x007FFFFF

_EV_U, _EV_GB, _EV_GC, _EV_Q, _EV_K, _EV_V, _EV_QI, _EV_KI, _EV_END = (
    0, 512, 1024, 1536, 2048, 2176, 2304, 2816, 2944)
_OD_Q, _OD_K, _OD_V, _OD_O, _OD_END = 0, 512, 1024, 2048, 3072


def _cparams(sem, vmem=VMEM_LIMIT):
    return pltpu.CompilerParams(dimension_semantics=sem, vmem_limit_bytes=vmem)


def _rms(x, g):
    return x * lax.rsqrt(jnp.mean(x * x, axis=-1, keepdims=True) + NORM_EPS) * g


def _dot(a, b):
    return jnp.dot(a, b, preferred_element_type=F32)


def _dot_nt(a, b):
    return lax.dot_general(a, b, (((1,), (1,)), ((), ())), preferred_element_type=F32)


def _rope128(x, c, sa, sb):
    return x * c + pltpu.roll(x, LANES - 8, 1) * sa + pltpu.roll(x, 8, 1) * sb


def _rope_tables(pos):
    r = HD_B // 4
    half = r // 2
    freqs = ROPE_THETA ** (-jnp.arange(half, dtype=F32) * 2.0 / r)
    ang = pos.astype(F32)[:, None] * freqs[None, :]
    cos, sin = jnp.cos(ang), jnp.sin(ang)
    t = pos.shape[0]
    ones = jnp.ones((t, HD_B - r), F32)
    zeros = jnp.zeros((t, HD_B - r), F32)
    zh = jnp.zeros((t, half), F32)
    c = jnp.concatenate([cos, cos, ones], axis=1)
    sa = jnp.concatenate([-sin, zh, zeros], axis=1)
    sb = jnp.concatenate([zh, sin, zeros], axis=1)
    tile2 = lambda a: jnp.concatenate([a, a], axis=1)
    return tile2(c), tile2(sa), tile2(sb)


_Q_SCALE = HD_B ** -0.5 * math.log2(math.e)
_ONES_ROWS = 16


def _rope128_t(x, c, sa, sb):
    return x * c + pltpu.roll(x, LANES - 8, 0) * sa + pltpu.roll(x, 8, 0) * sb


def _even_proj_body(x_ref, g_ref, w_ref, wwi_ref, cw_ref, cos_ref, sa_ref, sb_ref, init_ref,
                    ya_ref, q_ref, k_ref, v_ref, qi_ref, ki_ref, wit_ref, conv_ref, *, time_major_b):
    tm = x_ref.shape[0]
    xn = _rms(x_ref[...], g_ref[...]).astype(BF16)
    c, sa, sb = cos_ref[...], sa_ref[...], sb_ref[...]

    def seg(a, b):
        return _dot(xn, w_ref[:, a:b])

    z = seg(_EV_GC, _EV_Q) * seg(_EV_U, _EV_GB)
    cw = cw_ref[...]
    nb = time_major_b
    init = init_ref[...]
    z1 = jnp.concatenate([init[nb:2 * nb], z[:tm - nb]], axis=0)
    z2 = jnp.concatenate([init, z[:tm - 2 * nb]], axis=0)
    conv_ref[...] = z[tm - 2 * nb:, :]
    conv = cw[0:1] * z2 + cw[1:2] * z1 + cw[2:3] * z
    ya_ref[...] = (seg(_EV_GB, _EV_GC) * conv).astype(BF16)

    for j in range(4):
        a = _EV_Q + j * LANES
        q_ref[:, j * LANES:(j + 1) * LANES] = (
            _rope128(seg(a, a + LANES), c, sa, sb) * _Q_SCALE).astype(BF16)
        a = _EV_QI + j * LANES
        qi_ref[:, j * LANES:(j + 1) * LANES] = (
            _rope128(seg(a, a + LANES), c, sa, sb) * IDX_DIM ** -0.5).astype(BF16)
    k_ref[...] = _rope128(seg(_EV_K, _EV_V), c, sa, sb)
    v_ref[...] = seg(_EV_V, _EV_QI)
    ki_ref[...] = _rope128(seg(_EV_KI, _EV_END), c, sa, sb)[:, :IDX_DIM]
    wit_ref[...] = _dot_nt(wwi_ref[...], xn) * N_IDX_HEADS ** -0.5


def _even_proj_prompt_body(x_ref, g_ref, w_ref, wt_ref, wwi_ref, cw_ref, cos_ref, sa_ref, sb_ref,
                           cost_ref, sat_ref, sbt_ref, init_ref,
                           ya_ref, k_ref, v_ref, ki_ref, kb_ref, kib_ref, qt_ref, qit_ref, vt_ref,
                           wit_ref, conv_ref, carry_ref, *, tiles_per_seq):
    tm = x_ref.shape[0]
    xn = _rms(x_ref[...], g_ref[...]).astype(BF16)
    c, sa, sb = cos_ref[...], sa_ref[...], sb_ref[...]
    ct, sat, sbt = cost_ref[...], sat_ref[...], sbt_ref[...]

    def seg(a, b):
        return _dot(xn, w_ref[:, a:b])

    z = seg(_EV_GC, _EV_Q) * seg(_EV_U, _EV_GB)
    cw = cw_ref[...]

    @pl.when(pl.program_id(0) % tiles_per_seq == 0)
    def _():
        carry_ref[0:2, :] = init_ref[0]

    c0 = carry_ref[0:1, :]
    c1 = carry_ref[1:2, :]
    row = lax.broadcasted_iota(I32, (tm, 1), 0)
    z1 = jnp.where(row == 0, c1, pltpu.roll(z, 1, 0))
    z2 = jnp.where(row == 0, c0, jnp.where(row == 1, c1, pltpu.roll(z, 2, 0)))
    carry_ref[0:2, :] = z[tm - 2:tm, :]
    conv_ref[0] = z[tm - 2:tm, :]
    conv = cw[0:1] * z2 + cw[1:2] * z1 + cw[2:3] * z
    ya_ref[...] = (seg(_EV_GB, _EV_GC) * conv).astype(BF16)

    k = _rope128(seg(_EV_K, _EV_V), c, sa, sb)
    k_ref[...] = k
    kb_ref[...] = k.astype(BF16)
    v_ref[...] = seg(_EV_V, _EV_QI)
    ki = _rope128(seg(_EV_KI, _EV_END), c, sa, sb)[:, :IDX_DIM]
    ki_ref[...] = ki
    kib_ref[...] = ki.astype(BF16)

    nq = N_HEADS_B * HD_B
    nqi = N_IDX_HEADS * IDX_DIM
    for j in range(nq // LANES):
        xt = _dot_nt(wt_ref[j * LANES:(j + 1) * LANES, :], xn)
        qt_ref[j * LANES:(j + 1) * LANES, :] = (_rope128_t(xt, ct, sat, sbt) * _Q_SCALE).astype(BF16)
    for j in range(nqi // LANES):
        xt = _dot_nt(wt_ref[nq + j * LANES:nq + (j + 1) * LANES, :], xn)
        qit_ref[j * LANES:(j + 1) * LANES, :] = (
            _rope128_t(xt, ct, sat, sbt) * IDX_DIM ** -0.5).astype(BF16)
    vt = _dot_nt(wt_ref[nq + nqi:, :], xn).astype(BF16)
    kc = vt_ref.shape[3]
    ones = jnp.ones((_ONES_ROWS, kc), BF16)
    for j in range(tm // kc):
        for g in range(N_KV_B):
            vt_ref[j, g] = jnp.concatenate(
                [vt[g * HD_B:(g + 1) * HD_B, j * kc:(j + 1) * kc], ones], axis=0)
    wit_ref[...] = _dot_nt(wwi_ref[...], xn) * N_IDX_HEADS ** -0.5


def _even_proj_prompt(x, g, w, wt, wwi, cw, tabs, tabs_t, init, *, tm, seq_len, kc):
    m = x.shape[0]
    tiles_per_seq = seq_len // tm
    row = lambda i: (i, 0)
    col = lambda i: (0, i)
    const = lambda i: (0, 0)
    seq = lambda i: (i // tiles_per_seq, 0, 0)
    nq = N_HEADS_B * HD_B
    nqi = N_IDX_HEADS * IDX_DIM
    nkv = N_KV_B * HD_B
    tab_spec = pl.BlockSpec((tm, LANES), lambda i: (i % tiles_per_seq, 0))
    tabt_spec = pl.BlockSpec((LANES, tm), lambda i: (0, i % tiles_per_seq))
    vrows = HD_B + _ONES_ROWS
    in_specs = [
        pl.BlockSpec((tm, D_MODEL), row), pl.BlockSpec((1, D_MODEL), const),
        pl.BlockSpec((D_MODEL, _EV_END), const), pl.BlockSpec(wt.shape, const),
        pl.BlockSpec((N_IDX_HEADS, D_MODEL), const), pl.BlockSpec((CONV_W, D_CONV), const),
        tab_spec, tab_spec, tab_spec, tabt_spec, tabt_spec, tabt_spec,
        pl.BlockSpec((1, 2, D_CONV), seq),
    ]
    out_specs = (
        pl.BlockSpec((tm, D_CONV), row),
        pl.BlockSpec((tm, nkv), row), pl.BlockSpec((tm, nkv), row), pl.BlockSpec((tm, IDX_DIM), row),
        pl.BlockSpec((tm, nkv), row), pl.BlockSpec((tm, IDX_DIM), row),
        pl.BlockSpec((nq, tm), col), pl.BlockSpec((nqi, tm), col),
        pl.BlockSpec((tm // kc, N_KV_B, vrows, kc), lambda i: (i, 0, 0, 0)),
        pl.BlockSpec((N_IDX_HEADS, tm), col),
        pl.BlockSpec((1, 2, D_CONV), seq),
    )
    out_shape = (
        jax.ShapeDtypeStruct((m, D_CONV), BF16),
        jax.ShapeDtypeStruct((m, nkv), F32), jax.ShapeDtypeStruct((m, nkv), F32),
        jax.ShapeDtypeStruct((m, IDX_DIM), F32),
        jax.ShapeDtypeStruct((m, nkv), BF16), jax.ShapeDtypeStruct((m, IDX_DIM), BF16),
        jax.ShapeDtypeStruct((nq, m), BF16), jax.ShapeDtypeStruct((nqi, m), BF16),
        jax.ShapeDtypeStruct((m // kc, N_KV_B, vrows, kc), BF16),
        jax.ShapeDtypeStruct((N_IDX_HEADS, m), F32),
        jax.ShapeDtypeStruct((m // seq_len, 2, D_CONV), F32),
    )
    return pl.pallas_call(
        functools.partial(_even_proj_prompt_body, tiles_per_seq=tiles_per_seq),
        grid=(m // tm,), in_specs=in_specs, out_specs=out_specs, out_shape=out_shape,
        scratch_shapes=[pltpu.VMEM((SUBLANES, D_CONV), F32)],
        compiler_params=_cparams(("arbitrary",)), name="even_proj_prompt",
    )(x, g, w, wt, wwi, cw, *tabs, *tabs_t, init)


def _even_proj(x, g, w, wwi, cw, tabs, init, *, seq_len, time_major_b):
    m = x.shape[0]
    tm = m
    nt = 1
    assert seq_len >= 2
    row = lambda i: (i, 0)
    const = lambda i: (0, 0)
    tab_spec = pl.BlockSpec((tm, LANES), const)
    init_spec = pl.BlockSpec((2 * time_major_b, D_CONV), const)
    conv_shape = jax.ShapeDtypeStruct((2 * time_major_b, D_CONV), F32)
    conv_spec = pl.BlockSpec((2 * time_major_b, D_CONV), const)
    scratch = []
    out_shape = (
        jax.ShapeDtypeStruct((m, D_CONV), BF16),
        jax.ShapeDtypeStruct((m, N_HEADS_B * HD_B), BF16),
        jax.ShapeDtypeStruct((m, N_KV_B * HD_B), F32),
        jax.ShapeDtypeStruct((m, N_KV_B * HD_B), F32),
        jax.ShapeDtypeStruct((m, N_IDX_HEADS * IDX_DIM), BF16),
        jax.ShapeDtypeStruct((m, IDX_DIM), F32),
        jax.ShapeDtypeStruct((N_IDX_HEADS, m), F32),
        conv_shape,
    )
    out_specs = (
        pl.BlockSpec((tm, D_CONV), row),
        pl.BlockSpec((tm, N_HEADS_B * HD_B), row),
        pl.BlockSpec((tm, N_KV_B * HD_B), row),
        pl.BlockSpec((tm, N_KV_B * HD_B), row),
        pl.BlockSpec((tm, N_IDX_HEADS * IDX_DIM), row),
        pl.BlockSpec((tm, IDX_DIM), row),
        pl.BlockSpec((N_IDX_HEADS, tm), lambda i: (0, i)),
        conv_spec,
    )
    in_specs = [
        pl.BlockSpec((tm, D_MODEL), row),
        pl.BlockSpec((1, D_MODEL), const),
        pl.BlockSpec((D_MODEL, _EV_END), const),
        pl.BlockSpec((N_IDX_HEADS, D_MODEL), const),
        pl.BlockSpec((CONV_W, D_CONV), const),
        tab_spec, tab_spec, tab_spec,
        init_spec,
    ]
    return pl.pallas_call(
        functools.partial(_even_proj_body, time_major_b=time_major_b),
        grid=(nt,), in_specs=in_specs, out_specs=out_specs, out_shape=out_shape,
        scratch_shapes=scratch, compiler_params=_cparams(("arbitrary",)),
        name="even_proj",
    )(x, g, w, wwi, cw, *tabs, init)


def _mix_out_body(*refs, n_in):
    ins = refs[:n_in]
    ws = refs[n_in:2 * n_in]
    x_ref, g1_ref, g2_ref, wq_ref, x1_ref, qx_ref = refs[2 * n_in:]
    y = _dot(ins[0][...], ws[0][...])
    for a, w in zip(ins[1:], ws[1:]):
        y = y + _dot(a[...], w[...])
    x1 = x_ref[...] + _rms(y, g1_ref[...])
    x1_ref[...] = x1
    xn = _rms(x1, g2_ref[...]).astype(BF16)
    qx_ref[...] = (_dot(xn, wq_ref[...]) * HD_X ** -0.5).astype(BF16)


def _mix_out(ins, ws, x, g1, g2, wq, *, tm):
    m = x.shape[0]
    row = lambda i: (i, 0)
    const = lambda i: (0, 0)
    n_in = len(ins)
    in_specs = ([pl.BlockSpec((tm, a.shape[1]), row) for a in ins]
                + [pl.BlockSpec(w.shape, const) for w in ws]
                + [pl.BlockSpec((tm, D_MODEL), row), pl.BlockSpec((1, D_MODEL), const),
                   pl.BlockSpec((1, D_MODEL), const), pl.BlockSpec(wq.shape, const)])
    nq = wq.shape[1]
    return pl.pallas_call(
        functools.partial(_mix_out_body, n_in=n_in),
        grid=(m // tm,), in_specs=in_specs,
        out_specs=(pl.BlockSpec((tm, D_MODEL), row), pl.BlockSpec((tm, nq), row)),
        out_shape=(jax.ShapeDtypeStruct((m, D_MODEL), F32), jax.ShapeDtypeStruct((m, nq), BF16)),
        compiler_params=_cparams(("parallel",)), name="mix_out",
    )(*ins, *ws, x, g1, g2, wq)


def _mem_kv_body(mem_ref, g_ref, wk_ref, wv_ref, mk_ref, mv_ref):
    mn = _rms(mem_ref[...], g_ref[0]).astype(BF16)
    mk_ref[0] = _dot(mn, wk_ref[0])
    mv_ref[0] = _dot(mn, wv_ref[0])


def _mem_kv(mem, g, wk, wv, *, tm):
    m = mem.shape[0]
    depth = g.shape[0]
    n = wk.shape[2]
    wspec = pl.BlockSpec((1, D_MODEL, n), lambda l, i: (l, 0, 0))
    ospec = pl.BlockSpec((1, tm, n), lambda l, i: (l, i, 0))
    oshape = jax.ShapeDtypeStruct((depth, m, n), F32)
    return pl.pallas_call(
        _mem_kv_body, grid=(depth, m // tm),
        in_specs=[pl.BlockSpec((tm, D_MODEL), lambda l, i: (i, 0)),
                  pl.BlockSpec((1, 1, D_MODEL), lambda l, i: (l, 0, 0)), wspec, wspec],
        out_specs=(ospec, ospec), out_shape=(oshape, oshape),
        compiler_params=_cparams(("parallel", "parallel")), name="mem_kv",
    )(mem, g, wk, wv)


def _xattn_body(q_ref, mk_ref, mv_ref, x_ref, wo_ref, g_ref, o_ref):
    q = q_ref[0]
    mk = mk_ref[0].astype(BF16)
    mv = mv_ref[0].astype(BF16)
    outs = []
    for h in range(N_HEADS_X):
        sl = slice(h * HD_X, (h + 1) * HD_X)
        s = _dot_nt(q[:, sl], mk[:, sl])
        s = s - jnp.max(s, axis=-1, keepdims=True)
        p = jnp.exp(s)
        p = p / jnp.sum(p, axis=-1, keepdims=True)
        outs.append(_dot(p.astype(BF16), mv[:, sl]))
    o = jnp.concatenate(outs, axis=-1).astype(BF16)
    o_ref[0] = x_ref[0] + _rms(_dot(o, wo_ref[...]), g_ref[...])


def _xattn(q, mk, mv, x, wo, g, *, tm):
    b, t, _ = x.shape
    nq = q.shape[2]
    tile = lambda i, j: (i, j, 0)
    per_b = lambda i, j: (i, 0, 0)
    const = lambda i, j: (0, 0)
    return pl.pallas_call(
        _xattn_body, grid=(b, t // tm),
        in_specs=[pl.BlockSpec((1, tm, nq), tile),
                  pl.BlockSpec((1, N_MEM, nq), per_b), pl.BlockSpec((1, N_MEM, nq), per_b),
                  pl.BlockSpec((1, tm, D_MODEL), tile),
                  pl.BlockSpec(wo.shape, const), pl.BlockSpec((1, D_MODEL), const)],
        out_specs=pl.BlockSpec((1, tm, D_MODEL), tile),
        out_shape=jax.ShapeDtypeStruct((b, t, D_MODEL), F32),
        compiler_params=_cparams(("parallel", "parallel")), name="xattn",
    )(q, mk, mv, x, wo, g)


def _mlp_body(x_ref, g4_ref, g5_ref, w1_ref, w2_ref, o_ref, xn_ref, acc_ref):
    k = pl.program_id(1)

    @pl.when(k == 0)
    def _():
        xn_ref[...] = _rms(x_ref[...], g4_ref[...]).astype(BF16)

    h = jnp.maximum(_dot(xn_ref[...], w1_ref[...]), 0.0)
    part = _dot((h * h).astype(BF16), w2_ref[...])

    @pl.when(k == 0)
    def _():
        acc_ref[...] = part

    @pl.when(k > 0)
    def _():
        acc_ref[...] += part

    @pl.when(k == pl.num_programs(1) - 1)
    def _():
        o_ref[...] = x_ref[...] + _rms(acc_ref[...], g5_ref[...])


def _mlp(x, g4, g5, w1, w2, *, tm, tf):
    m = x.shape[0]
    f = w1.shape[1]
    row = lambda i, k: (i, 0)
    const = lambda i, k: (0, 0)
    return pl.pallas_call(
        _mlp_body, grid=(m // tm, f // tf),
        in_specs=[pl.BlockSpec((tm, D_MODEL), row), pl.BlockSpec((1, D_MODEL), const),
                  pl.BlockSpec((1, D_MODEL), const),
                  pl.BlockSpec((D_MODEL, tf), lambda i, k: (0, k)),
                  pl.BlockSpec((tf, D_MODEL), lambda i, k: (k, 0))],
        out_specs=pl.BlockSpec((tm, D_MODEL), row),
        out_shape=jax.ShapeDtypeStruct((m, D_MODEL), F32),
        scratch_shapes=[pltpu.VMEM((tm, D_MODEL), BF16), pltpu.VMEM((tm, D_MODEL), F32)],
        compiler_params=_cparams(("parallel", "arbitrary")), name="mlp",
    )(x, g4, g5, w1, w2)


def _sortable_key(score):
    bits = pltpu.bitcast(score, I32)
    return bits ^ ((bits >> 31) & 0x7FFFFFFF)


def _dsa_prompt_body(qit_ref, wit_ref, ki_ref, qt_ref, k_ref, vt_ref, o_ref,
                     key_ref, qpad_ref, m_ref, acc_ref, *, ktop, idx_bits):
    tq = qit_ref.shape[1]
    kc = tq
    i = pl.program_id(1)
    nck = i + 1
    kiota = lax.broadcasted_iota(I32, (kc, tq), 0)
    qidx = i * tq + lax.broadcasted_iota(I32, (kc, tq), 1)
    ng = N_HEADS_B // N_KV_B

    def chunk_off(c):
        return pl.multiple_of(c * kc, kc)

    def score_chunk(c, carry):
        off = chunk_off(c)
        kic = ki_ref[pl.ds(off, kc), :]
        sc = jnp.zeros((kc, tq), F32)
        for h in range(N_IDX_HEADS):
            s = _dot(kic, qit_ref[h * IDX_DIM:(h + 1) * IDX_DIM, :])
            sc = sc + wit_ref[h:h + 1, :] * jnp.maximum(s, 0.0)
        sc = jnp.where(kiota + off <= qidx, sc, -jnp.inf)
        key_ref[pl.ds(off, kc), :] = _sortable_key(sc)
        return carry

    lax.fori_loop(0, nck, score_chunk, 0)

    def count(pred):
        def body(c, acc):
            off = chunk_off(c)
            hit = pred(key_ref[pl.ds(off, kc), :], kiota + off).astype(I32)
            return acc + jnp.sum(hit.reshape(kc // SUBLANES, SUBLANES, tq), axis=0)

        acc = lax.fori_loop(0, nck, body, jnp.zeros((SUBLANES, tq), I32))
        return jnp.sum(acc, axis=0, keepdims=True)

    def bit_step(j, lo):
        cand = lo + jnp.left_shift(jnp.int32(1), 31 - j)
        cnt = count(lambda blk, _: blk >= cand)
        return jnp.where(cnt >= ktop, cand, lo)

    thr = lax.fori_loop(0, 32, bit_step, jnp.full((1, tq), INT_MIN, I32))

    cnt_gt = count(lambda blk, _: blk > thr)
    cnt_ge = count(lambda blk, _: blk >= thr)
    tie = jnp.logical_and(thr > KEY_NEG_INF, cnt_ge > ktop)

    @pl.when(jnp.max(tie.astype(I32)) > 0)
    def _():
        need = ktop - cnt_gt

        def idx_step(j, res):
            cand = res + jnp.left_shift(jnp.int32(1), idx_bits - 1 - j)
            cnt = count(lambda blk, kidx: jnp.logical_and(blk == thr, kidx < cand))
            return jnp.where(cnt < need, cand, res)

        jcut = lax.fori_loop(0, idx_bits, idx_step, jnp.zeros((1, tq), I32))

        def drop_chunk(c, carry):
            off = chunk_off(c)
            blk = key_ref[pl.ds(off, kc), :]
            drop = jnp.logical_and(blk == thr, kiota + off > jcut)
            key_ref[pl.ds(off, kc), :] = jnp.where(drop, blk - 1, blk)
            return carry

        lax.fori_loop(0, nck, drop_chunk, 0)

    thr_eff = jnp.maximum(thr, KEY_NEG_INF + 1)
    zero_half = jnp.zeros((HD_B, tq), BF16)
    for g in range(N_KV_B):
        cols = []
        for hh in range(ng):
            h = g * ng + hh
            qh = qt_ref[h * HD_B:(h + 1) * HD_B, :]
            halves = [zero_half] * N_KV_B
            halves[g] = qh
            cols.append(jnp.concatenate(halves, axis=0))
        qpad_ref[g] = jnp.concatenate(cols, axis=1)
    m_ref[...] = jnp.full(m_ref.shape, NEG_BIG, F32)
    acc_ref[...] = jnp.zeros(acc_ref.shape, F32)

    def attn_chunk(c, carry):
        off = chunk_off(c)
        bias = jnp.where(key_ref[pl.ds(off, kc), :] >= thr_eff, 0.0, NEG_BIG)
        bias = jnp.concatenate([bias] * ng, axis=1)
        kch = k_ref[pl.ds(off, kc), :]
        for g in range(N_KV_B):
            s = _dot(kch, qpad_ref[g]) + bias
            m_old = m_ref[g]
            m_new = jnp.maximum(m_old, jnp.max(s, axis=0, keepdims=True))
            alpha = jnp.exp2(m_old - m_new)
            p = jnp.exp2(s - m_new).astype(BF16)
            acc_ref[g] = alpha * acc_ref[g] + _dot(vt_ref[c, g], p)
            m_ref[g] = m_new
        return carry

    lax.fori_loop(0, nck, attn_chunk, 0)
    for g in range(N_KV_B):
        acc = acc_ref[g]
        o = acc[:HD_B] / acc[HD_B:HD_B + 1]
        for hp in range(ng // 2):
            pair = jnp.concatenate([o[:, (2 * hp) * tq:(2 * hp + 1) * tq],
                                    o[:, (2 * hp + 1) * tq:(2 * hp + 2) * tq]], axis=0)
            lane0 = (g * ng + 2 * hp) * HD_B
            o_ref[:, lane0:lane0 + 2 * HD_B] = pair.T.astype(BF16)


def _dsa_prompt(qit, wit, kib, qt, kb, vt, *, b, t, tq):
    hd = qt.shape[0]
    nt = t // tq
    ktop = min(TOPK_MAX, t // 4)
    idx_bits = max(1, (t - 1).bit_length())
    ng = N_HEADS_B // N_KV_B
    vrows = vt.shape[2]
    col = lambda bi, i: (0, bi * nt + i)
    return pl.pallas_call(
        functools.partial(_dsa_prompt_body, ktop=ktop, idx_bits=idx_bits),
        grid=(b, nt),
        in_specs=[
            pl.BlockSpec((hd, tq), col),
            pl.BlockSpec((N_IDX_HEADS, tq), col),
            pl.BlockSpec((t, IDX_DIM), lambda bi, i: (bi, 0)),
            pl.BlockSpec((hd, tq), col),
            pl.BlockSpec((t, N_KV_B * HD_B), lambda bi, i: (bi, 0)),
            pl.BlockSpec((nt, N_KV_B, vrows, tq), lambda bi, i: (bi, 0, 0, 0)),
        ],
        out_specs=pl.BlockSpec((tq, hd), lambda bi, i: (bi * nt + i, 0)),
        out_shape=jax.ShapeDtypeStruct((b * t, hd), BF16),
        scratch_shapes=[pltpu.VMEM((t, tq), I32),
                        pltpu.VMEM((N_KV_B, N_KV_B * HD_B, ng * tq), BF16),
                        pltpu.VMEM((N_KV_B, 1, ng * tq), F32),
                        pltpu.VMEM((N_KV_B, vrows, ng * tq), F32)],
        compiler_params=_cparams(("parallel", "arbitrary")), name="dsa_prompt",
    )(qit, wit, kib, qt, kb, vt)


_TP = SUBLANES
_NB = LANES
_PG_MAX = 16


def _dsa_sample_body(pt_ref, qi_ref, w_ref, kin_ref, q_ref, kn_ref, vn_ref,
                     ci_hbm, ck_hbm, cv_hbm, o_ref,
                     ibuf, kbuf, vbuf, key_ref, keyn_ref, isem, ksem, vsem,
                     *, n_pages, pg, t_new, ktop, idx_bits):
    b = pl.program_id(0)
    slot = b % 2
    ck = pg * PAGE_SIZE
    nch = n_pages // pg
    past = n_pages * PAGE_SIZE

    def idx_copy(sl, p, page):
        return pltpu.make_async_copy(ci_hbm.at[page], ibuf.at[sl, p], isem.at[sl])

    def k_copy(p, page):
        return pltpu.make_async_copy(ck_hbm.at[page], kbuf.at[p], ksem)

    def v_copy(p, page):
        return pltpu.make_async_copy(cv_hbm.at[page], vbuf.at[p], vsem)

    def issue_idx(bb, sl):
        def f(p, carry):
            idx_copy(sl, p, pt_ref[bb, p]).start()
            return carry

        lax.fori_loop(0, n_pages, f, 0)

    @pl.when(b == 0)
    def _():
        issue_idx(0, 0)

    def issue_kv(p, carry):
        page = pt_ref[b, p]
        k_copy(p, page).start()
        v_copy(p, page).start()
        return carry

    lax.fori_loop(0, n_pages, issue_kv, 0)

    @pl.when(b + 1 < pl.num_programs(0))
    def _():
        issue_idx(b + 1, 1 - slot)

    def wait_pages(make):
        def w(p, carry):
            make(p).wait()
            return carry

        lax.fori_loop(0, n_pages, w, 0)

    lane = lax.broadcasted_iota(I32, (_TP, ck), 1)
    lane_n = lax.broadcasted_iota(I32, (_TP, _NB), 1)
    row_n = lax.broadcasted_iota(I32, (_TP, _NB), 0)

    qi = qi_ref[0]
    nt = qi.shape[0] // N_IDX_HEADS

    def scores(kp):
        n = kp.shape[0]
        s = jnp.maximum(_dot_nt(qi, kp), 0.0) * w_ref[0]
        return jnp.sum(s.reshape(N_IDX_HEADS, nt, n), axis=0)

    wait_pages(lambda p: idx_copy(slot, p, 0))

    def sc_chunk(c, carry):
        kp = ibuf[slot, pl.ds(c * pg, pg)].reshape(ck, IDX_DIM).astype(BF16)
        key_ref[c] = _sortable_key(scores(kp))
        return carry

    lax.fori_loop(0, nch, sc_chunk, 0)
    admissible = jnp.logical_and(lane_n <= row_n, row_n < t_new)
    keyn_ref[...] = _sortable_key(jnp.where(admissible, scores(kin_ref[0]), -jnp.inf))

    def count(pred):
        def body(c, acc):
            return acc + pred(key_ref[c], lane + c * ck).astype(I32)

        acc = lax.fori_loop(0, nch, body, jnp.zeros((_TP, ck), I32))
        acc_n = pred(keyn_ref[...], lane_n + past).astype(I32)
        return jnp.sum(acc, axis=1, keepdims=True) + jnp.sum(acc_n, axis=1, keepdims=True)

    def bit_step(j, lo):
        cand = lo + jnp.left_shift(jnp.int32(1), 31 - j)
        cnt = count(lambda blk, _: blk >= cand)
        return jnp.where(cnt >= ktop, cand, lo)

    thr = lax.fori_loop(0, 32, bit_step, jnp.full((_TP, 1), INT_MIN, I32))
    cnt_gt = count(lambda blk, _: blk > thr)
    cnt_ge = count(lambda blk, _: blk >= thr)
    tie = jnp.logical_and(thr > KEY_NEG_INF, cnt_ge > ktop)

    @pl.when(jnp.max(tie.astype(I32)) > 0)
    def _():
        need = ktop - cnt_gt

        def idx_step(j, res):
            cand = res + jnp.left_shift(jnp.int32(1), idx_bits - 1 - j)
            cnt = count(lambda blk, kidx: jnp.logical_and(blk == thr, kidx < cand))
            return jnp.where(cnt < need, cand, res)

        jcut = lax.fori_loop(0, idx_bits, idx_step, jnp.zeros((_TP, 1), I32))

        def dropped(blk, kidx):
            return jnp.where(jnp.logical_and(blk == thr, kidx > jcut), blk - 1, blk)

        def drop_chunk(c, carry):
            key_ref[c] = dropped(key_ref[c], lane + c * ck)
            return carry

        lax.fori_loop(0, nch, drop_chunk, 0)
        keyn_ref[...] = dropped(keyn_ref[...], lane_n + past)

    thr_eff = jnp.maximum(thr, KEY_NEG_INF + 1)
    qp = q_ref[0]
    reps = qp.shape[0] // _TP

    def attend(keys, kc, vc, carry):
        m, l, acc = carry
        bias = jnp.where(keys >= thr_eff, 0.0, NEG_BIG)
        s = _dot_nt(qp, kc) + jnp.concatenate([bias] * reps, axis=0)
        m_new = jnp.maximum(m, jnp.max(s, axis=-1, keepdims=True))
        alpha = jnp.exp2(m - m_new)
        p = jnp.exp2(s - m_new)
        l = alpha * l + jnp.sum(p, axis=-1, keepdims=True)
        acc = alpha * acc + _dot(p.astype(BF16), vc)
        return m_new, l, acc

    wait_pages(lambda p: k_copy(p, 0))
    wait_pages(lambda p: v_copy(p, 0))

    def at_chunk(c, carry):
        kc = kbuf[pl.ds(c * pg, pg)].reshape(ck, N_KV_B * HD_B).astype(BF16)
        vc = vbuf[pl.ds(c * pg, pg)].reshape(ck, N_KV_B * HD_B).astype(BF16)
        return attend(key_ref[c], kc, vc, carry)

    nr = qp.shape[0]
    init = (jnp.full((nr, 1), NEG_BIG, F32), jnp.zeros((nr, 1), F32),
            jnp.zeros((nr, N_KV_B * HD_B), F32))
    carry = lax.fori_loop(0, nch, at_chunk, init)
    m, l, acc = attend(keyn_ref[...], kn_ref[0], vn_ref[0], carry)
    o_ref[0] = acc / l


def _dsa_sample(page_table, qi, w, kin, qp, kn, vn, cache_i, cache_k, cache_v, *, t_new):
    b, n_pages = page_table.shape
    pg = math.gcd(n_pages, _PG_MAX)
    nr = qp.shape[1]
    total = n_pages * PAGE_SIZE + t_new
    ktop = min(TOPK_MAX, total // 4)
    idx_bits = max(1, (total - 1).bit_length())
    nch = n_pages // pg
    per_b = lambda i, pt: (i, 0, 0)
    grid_spec = pltpu.PrefetchScalarGridSpec(
        num_scalar_prefetch=1, grid=(b,),
        in_specs=[pl.BlockSpec((1,) + qi.shape[1:], per_b), pl.BlockSpec((1,) + w.shape[1:], per_b),
                  pl.BlockSpec((1,) + kin.shape[1:], per_b), pl.BlockSpec((1,) + qp.shape[1:], per_b),
                  pl.BlockSpec((1,) + kn.shape[1:], per_b), pl.BlockSpec((1,) + vn.shape[1:], per_b),
                  pl.BlockSpec(memory_space=pl.ANY), pl.BlockSpec(memory_space=pl.ANY),
                  pl.BlockSpec(memory_space=pl.ANY)],
        out_specs=pl.BlockSpec((1, nr, N_KV_B * HD_B), per_b),
        scratch_shapes=[pltpu.VMEM((2, n_pages, PAGE_SIZE, IDX_DIM), F32),
                        pltpu.VMEM((n_pages, PAGE_SIZE, N_KV_B * HD_B), F32),
                        pltpu.VMEM((n_pages, PAGE_SIZE, N_KV_B * HD_B), F32),
                        pltpu.VMEM((nch, _TP, pg * PAGE_SIZE), I32),
                        pltpu.VMEM((_TP, _NB), I32),
                        pltpu.SemaphoreType.DMA((2,)),
                        pltpu.SemaphoreType.DMA(()),
                        pltpu.SemaphoreType.DMA(())])
    return pl.pallas_call(
        functools.partial(_dsa_sample_body, n_pages=n_pages, pg=pg, t_new=t_new, ktop=ktop,
                          idx_bits=idx_bits),
        grid_spec=grid_spec,
        out_shape=jax.ShapeDtypeStruct((b, nr, N_KV_B * HD_B), F32),
        compiler_params=_cparams(("arbitrary",)), name="dsa_sample",
    )(page_table, qi, w, kin, qp, kn, vn, cache_i, cache_k, cache_v)


def _log_sigmoid(x):
    return jnp.minimum(x, 0.0) - jnp.log1p(jnp.exp(-jnp.abs(x)))


def _odd_proj_body(x_ref, g_ref, w_ref, wt_ref, wg_ref, wgt_ref, bg_ref, bgt_ref,
                   qt_ref, k_ref, vt_ref, so_ref, gate_ref, gatet_ref):
    xn = _rms(x_ref[...], g_ref[...]).astype(BF16)
    nqk = N_HEADS_C * DQK_C
    k_ref[...] = _dot(xn, w_ref[:, :nqk]).astype(BF16)
    so_ref[...] = jax.nn.sigmoid(_dot(xn, w_ref[:, nqk:])).astype(BF16)
    qt_ref[...] = (_dot_nt(wt_ref[:nqk, :], xn) * DQK_C ** -0.5).astype(BF16)
    vt_ref[...] = _dot_nt(wt_ref[nqk:, :], xn).astype(BF16)
    nh = N_HEADS_C
    gate = _dot(xn, wg_ref[...]) + bg_ref[...]
    col = lax.broadcasted_iota(I32, gate.shape, 1)
    gate_ref[...] = jnp.where(col < nh, gate, _log_sigmoid(gate))
    gatet = _dot_nt(wgt_ref[...], xn) + bgt_ref[...]
    rowi = lax.broadcasted_iota(I32, gatet.shape, 0)
    gatet_ref[...] = jnp.where(rowi < nh, gatet, _log_sigmoid(gatet))


def _odd_proj(x, g, w, wt, wg, wgt, bg, bgt, *, tm):
    m = x.shape[0]
    row = lambda i: (i, 0)
    col = lambda i: (0, i)
    const = lambda i: (0, 0)
    nqk = N_HEADS_C * DQK_C
    nv = N_HEADS_C * DV_C
    ng = 2 * N_HEADS_C
    return pl.pallas_call(
        _odd_proj_body, grid=(m // tm,),
        in_specs=[pl.BlockSpec((tm, D_MODEL), row), pl.BlockSpec((1, D_MODEL), const),
                  pl.BlockSpec(w.shape, const), pl.BlockSpec(wt.shape, const),
                  pl.BlockSpec((D_MODEL, ng), const),
                  pl.BlockSpec((ng, D_MODEL), const), pl.BlockSpec((1, ng), const),
                  pl.BlockSpec((ng, 1), const)],
        out_specs=(pl.BlockSpec((nqk, tm), col), pl.BlockSpec((tm, nqk), row),
                   pl.BlockSpec((nv, tm), col), pl.BlockSpec((tm, nv), row),
                   pl.BlockSpec((tm, ng), row), pl.BlockSpec((ng, tm), col)),
        out_shape=(jax.ShapeDtypeStruct((nqk, m), BF16), jax.ShapeDtypeStruct((m, nqk), BF16),
                   jax.ShapeDtypeStruct((nv, m), BF16), jax.ShapeDtypeStruct((m, nv), BF16),
                   jax.ShapeDtypeStruct((m, ng), F32), jax.ShapeDtypeStruct((ng, m), F32)),
        compiler_params=_cparams(("parallel",)), name="odd_proj",
    )(x, g, w, wt, wg, wgt, bg, bgt)


def _split3(x):
    hi = x.astype(BF16)
    r1 = x - hi.astype(F32)
    mid = r1.astype(BF16)
    lo = (r1 - mid.astype(F32)).astype(BF16)
    return hi, mid, lo


def _mlstm_body(qt_ref, k_ref, vt_ref, so_ref, gate_ref, gatet_ref, hg_ref,
                c0_ref, n0_ref, m0_ref,
                h_ref, c_ref, n_ref, m_ref, cs_ref, ns_ref, ms_ref):
    lc = k_ref.shape[0]
    nh = N_HEADS_C
    c = pl.program_id(1)

    @pl.when(c == 0)
    def _():
        cs_ref[...] = c0_ref[0]
        ns_ref[...] = n0_ref[0]
        ms_ref[...] = jnp.broadcast_to(m0_ref[0], ms_ref.shape)

    ri = lax.broadcasted_iota(I32, (lc, lc), 0)
    ci = lax.broadcasted_iota(I32, (lc, lc), 1)
    causal_t = ri <= ci
    tril = (ci <= ri).astype(BF16)
    triu = causal_t.astype(BF16)
    gate = gate_ref[...]
    gatet = gatet_ref[...]
    lf3 = _split3(gate[:, nh:])
    bcols = _dot(tril, lf3[0]) + _dot(tril, lf3[1]) + _dot(tril, lf3[2])
    lft3 = _split3(gatet[nh:, :])
    brows = _dot(lft3[0], triu) + _dot(lft3[1], triu) + _dot(lft3[2], triu)
    half_lane = lax.broadcasted_iota(I32, (1, 2 * DQK_C), 1) // DQK_C
    zero_q = jnp.zeros((DQK_C, lc), BF16)
    ms = ms_ref[...]
    ms_new = []
    for h in range(nh):
        j, half = divmod(h, 2)
        in_half = half_lane == half
        kp = k_ref[:, j * 2 * DQK_C:(j + 1) * 2 * DQK_C]
        qh = qt_ref[h * DQK_C:(h + 1) * DQK_C, :]
        qpad = jnp.concatenate([qh, zero_q] if half == 0 else [zero_q, qh], axis=0)
        vt = vt_ref[h * DV_C:(h + 1) * DV_C, :]
        br = brows[h:h + 1, :]
        igr = gatet[h:h + 1, :]
        m_prev = ms[h:h + 1, 0:1]
        a = br + m_prev
        src = gate[:, h:h + 1] - bcols[:, h:h + 1]
        d = jnp.where(causal_t, br + src, NEG_BIG)
        mj = jnp.maximum(a, jnp.max(d, axis=0, keepdims=True))
        s = _dot(kp, qpad) * jnp.exp(d - mj)
        aw = jnp.exp(a - mj)
        cp = cs_ref[j]
        n8 = jnp.broadcast_to(ns_ref[j:j + 1, :], (SUBLANES, 2 * DQK_C)).astype(BF16)
        num = _dot(vt, s.astype(BF16)) + aw * _dot(cp.astype(BF16), qpad)
        den = jnp.sum(s, axis=0, keepdims=True) + aw * _dot(n8, qpad)[0:1]
        ht = num / jnp.maximum(jnp.abs(den), jnp.exp(-mj))
        ht = ht * lax.rsqrt(jnp.mean(ht * ht, axis=0, keepdims=True) + NORM_EPS)
        sl = slice(h * DV_C, (h + 1) * DV_C)
        h_ref[:, sl] = (ht.T * hg_ref[:, sl] * so_ref[:, sl].astype(F32)).astype(BF16)
        b_last = br[:, lc - 1:lc]
        g_row = b_last - br + igr
        m_new = jnp.maximum(b_last + m_prev, jnp.max(g_row, axis=-1, keepdims=True))
        gw = jnp.exp(g_row - m_new)
        decay = jnp.exp(b_last + m_prev - m_new)
        upd = _dot((vt.astype(F32) * gw).astype(BF16), kp)
        cs_ref[j] = jnp.where(in_half, decay * cp + upd, cp)
        gw8 = jnp.broadcast_to(gw, (SUBLANES, lc)).astype(BF16)
        n_old = ns_ref[j:j + 1, :]
        ns_ref[j:j + 1, :] = jnp.where(in_half, decay * n_old + _dot(gw8, kp)[0:1], n_old)
        ms_new.append(jnp.broadcast_to(m_new, (1, ms.shape[1])))
    ms_ref[...] = jnp.concatenate(ms_new, axis=0)

    @pl.when(c == pl.num_programs(1) - 1)
    def _():
        c_ref[0] = cs_ref[...]
        n_ref[0] = ns_ref[...]
        m_ref[0] = ms_ref[...]


def _mlstm(qt, k, vt, so, gate, gatet, hg, c0p, n0p, m0, *, b, t, lc):
    nh = N_HEADS_C
    nqk = nh * DQK_C
    nv = nh * DV_C
    ng = gate.shape[1]
    nc = t // lc
    npair = nh // 2
    row = lambda bi, c: (bi * nc + c, 0)
    col = lambda bi, c: (0, bi * nc + c)
    per_b4 = lambda bi, c: (bi, 0, 0, 0)
    per_b3 = lambda bi, c: (bi, 0, 0)
    return pl.pallas_call(
        _mlstm_body, grid=(b, nc),
        in_specs=[pl.BlockSpec((nqk, lc), col), pl.BlockSpec((lc, nqk), row),
                  pl.BlockSpec((nv, lc), col), pl.BlockSpec((lc, nv), row),
                  pl.BlockSpec((lc, ng), row), pl.BlockSpec((ng, lc), col),
                  pl.BlockSpec((1, nv), lambda bi, c: (0, 0)),
                  pl.BlockSpec((1, npair, DV_C, 2 * DQK_C), per_b4),
                  pl.BlockSpec((1, npair, 2 * DQK_C), per_b3),
                  pl.BlockSpec((1, nh, 1), per_b3)],
        out_specs=(pl.BlockSpec((lc, nv), row),
                   pl.BlockSpec((1, npair, DV_C, 2 * DQK_C), per_b4),
                   pl.BlockSpec((1, npair, 2 * DQK_C), per_b3),
                   pl.BlockSpec((1, nh, LANES), per_b3)),
        out_shape=(jax.ShapeDtypeStruct((b * t, nv), BF16),
                   jax.ShapeDtypeStruct((b, npair, DV_C, 2 * DQK_C), F32),
                   jax.ShapeDtypeStruct((b, npair, 2 * DQK_C), F32),
                   jax.ShapeDtypeStruct((b, nh, LANES), F32)),
        scratch_shapes=[pltpu.VMEM((npair, DV_C, 2 * DQK_C), F32),
                        pltpu.VMEM((npair, 2 * DQK_C), F32), pltpu.VMEM((nh, LANES), F32)],
        compiler_params=_cparams(("parallel", "arbitrary")), name="mlstm",
    )(qt, k, vt, so, gate, gatet, hg, c0p, n0p, m0)


def _prep_even(w_in, w_out):
    sizes = [D_CONV, D_CONV, D_CONV, N_HEADS_B * HD_B, N_KV_B * HD_B, N_KV_B * HD_B,
             N_IDX_HEADS * IDX_DIM, N_IDX_HEADS]
    offs = [sum(sizes[:j + 1]) for j in range(len(sizes))]
    u, gb, gc, q, k, v, qi, wi, ki = jnp.split(w_in, offs, axis=1)
    pad = jnp.zeros((D_MODEL, LANES - IDX_DIM), w_in.dtype)
    w = jnp.concatenate([u, gb, gc, q, k, v, qi, ki, pad], axis=1).astype(BF16)
    wt = jnp.concatenate([q, qi, v], axis=1).T.astype(BF16)
    return (w, wt, wi.T.astype(BF16), w_out[:D_CONV].astype(BF16), w_out[D_CONV:].astype(BF16))


def _prep_odd(w_in, b_i, b_f):
    q, k, v, o, wg = jnp.split(w_in, [_OD_K, _OD_V, _OD_O, _OD_END], axis=1)
    w = jnp.concatenate([k, o], axis=1).astype(BF16)
    wt = jnp.concatenate([q, v], axis=1).T.astype(BF16)
    wg = wg.astype(BF16)
    bg = jnp.concatenate([b_i, b_f])
    return w, wt, wg, wg.T, bg[None, :], bg[:, None]


def _tile(n, pref):
    return pref if n % pref == 0 else n


def _pad_axis(a, axis, n):
    if a.shape[axis] == n:
        return a
    widths = [(0, 0)] * a.ndim
    widths[axis] = (0, n - a.shape[axis])
    return jnp.pad(a, widths)


def _tail(x2, qx, mk, mv, wo, g3, g4, g5, w1, w2, *, b, t):
    t_pad = max(t, 2 * SUBLANES)
    tx = _tile(t_pad, 512)
    q3 = _pad_axis(qx.reshape(b, t, -1), 1, t_pad)
    x3 = _pad_axis(x2.reshape(b, t, D_MODEL), 1, t_pad)
    x3 = _xattn(q3, mk, mv, x3, wo, g3, tm=tx)[:, :t].reshape(b * t, D_MODEL)
    return _mlp(x3, g4, g5, w1, w2, tm=_tile(b * t, 1024), tf=1024)


def _even_prompt(x2, g, we, cw, wxq, *, b, t, tm):
    w, wt, wwi, wo_a, wo_b = we
    tq = _tile(t, 256)
    tabs = _rope_tables(jnp.arange(t))
    tabs_t = tuple(tab.T for tab in tabs)
    init = jnp.zeros((b, CONV_W - 1, D_CONV), F32)
    ya, k, v, ki, kb, kib, qt, qit, vt, wit, conv = _even_proj_prompt(
        x2, g[0:1], w, wt, wwi, cw, tabs, tabs_t, init, tm=tm, seq_len=t, kc=tq)
    yb = _dsa_prompt(qit, wit, kib, qt, kb, vt, b=b, t=t, tq=tq)
    x1, qx = _mix_out([ya, yb], [wo_a, wo_b], x2, g[1:2], g[2:3], wxq, tm=tm)
    state = (k.reshape(b, t, N_KV_B, HD_B), v.reshape(b, t, N_KV_B, HD_B),
             ki.reshape(b, t, IDX_DIM), conv)
    return x1, qx, state


def _odd_mixer(x2, g, wod, hg, w_out, wxq, c0, n0, m0, *, b, t, tm, lc):
    w, wt, wg, wgt, bg, bgt = wod
    qt, k, vt, so, gate, gatet = _odd_proj(x2, g[0:1], w, wt, wg, wgt, bg, bgt, tm=tm)
    nh = N_HEADS_C
    tp = -(-t // lc) * lc
    if tp > t:
        rows = lambda a: _pad_axis(a.reshape(b, t, -1), 1, tp).reshape(b * tp, -1)
        cols = lambda a: _pad_axis(a.reshape(-1, b, t), 2, tp).reshape(-1, b * tp)
        neutral = jnp.concatenate([jnp.full((nh,), NEG_BIG, F32), jnp.zeros((nh,), F32)])
        gate = jnp.concatenate(
            [gate.reshape(b, t, 2 * nh),
             jnp.broadcast_to(neutral[None, None, :], (b, tp - t, 2 * nh))], axis=1)
        gatet = jnp.concatenate(
            [gatet.reshape(2 * nh, b, t),
             jnp.broadcast_to(neutral[:, None, None], (2 * nh, b, tp - t))], axis=2)
        qt, k, vt, so = cols(qt), rows(k), cols(vt), rows(so)
        gate, gatet = gate.reshape(b * tp, 2 * nh), gatet.reshape(2 * nh, b * tp)
    c0p = c0.reshape(b, nh // 2, 2, DV_C, DQK_C).transpose(0, 1, 3, 2, 4)
    c0p = c0p.reshape(b, nh // 2, DV_C, 2 * DQK_C)
    h, cp, n, m = _mlstm(qt, k, vt, so, gate, gatet, hg, c0p, n0.reshape(b, nh // 2, 2 * DQK_C),
                         m0[:, :, None], b=b, t=tp, lc=lc)
    h2 = h.reshape(b, tp, nh * DV_C)[:, :t].reshape(b * t, nh * DV_C)
    x1, qx = _mix_out([h2], [w_out], x2, g[1:2], g[2:3], wxq, tm=tm)
    c_out = cp.reshape(b, nh // 2, DV_C, 2, DQK_C).transpose(0, 1, 3, 2, 4)
    c_out = c_out.reshape(b, nh, DV_C, DQK_C)
    return x1, qx, (c_out, n.reshape(b, nh, DQK_C), m[:, :, 0])


def _even_sample(x2, g, we, cw, wxq, cache_i, cache_k, cache_v, state_conv, page_table, *, b, t):
    w, _, wwi, wo_a, wo_b = we
    n_pages = page_table.shape[1]
    pos = n_pages * PAGE_SIZE + jnp.arange(t)
    tabs = tuple(jnp.repeat(tab, b, axis=0) for tab in _rope_tables(pos))
    to_tm = lambda a: a.reshape(b, t, -1).transpose(1, 0, 2).reshape(t * b, -1)
    to_bm = lambda a: a.reshape(t, b, -1).transpose(1, 0, 2)
    init = state_conv.transpose(1, 0, 2).reshape(2 * b, D_CONV)
    ya, q, k, v, qi, ki, wit, conv = _even_proj(
        to_tm(x2), g[0:1], w, wwi, cw, tabs, init, seq_len=t, time_major_b=b)
    ng = N_HEADS_B // N_KV_B
    qi_b = to_bm(qi).reshape(b, t, N_IDX_HEADS, IDX_DIM).transpose(0, 2, 1, 3)
    qi_b = _pad_axis(qi_b, 2, _TP).reshape(b, N_IDX_HEADS * _TP, IDX_DIM)
    w_b = _pad_axis(wit.reshape(N_IDX_HEADS, t, b).transpose(2, 0, 1), 2, _TP)
    w_b = w_b.reshape(b, N_IDX_HEADS * _TP, 1)
    kin = _pad_axis(to_bm(ki), 1, _NB).astype(BF16)
    kn = _pad_axis(to_bm(k), 1, _NB).astype(BF16)
    vn = _pad_axis(to_bm(v), 1, _NB).astype(BF16)
    q5 = _pad_axis(to_bm(q).reshape(b, t, N_KV_B, ng, HD_B).transpose(0, 2, 3, 1, 4), 3, _TP)
    zq = jnp.zeros_like(q5[:, 0])
    qp = jnp.stack([jnp.concatenate([q5[:, 0], zq], axis=-1),
                    jnp.concatenate([zq, q5[:, 1]], axis=-1)], axis=1)
    qp = qp.reshape(b, N_KV_B * ng * _TP, N_KV_B * HD_B)
    n_phys = cache_k.shape[0]
    o = _dsa_sample(page_table, qi_b, w_b, kin, qp, kn, vn,
                    cache_i, cache_k.reshape(n_phys, PAGE_SIZE, -1),
                    cache_v.reshape(n_phys, PAGE_SIZE, -1), t_new=t)
    o = o.reshape(b, N_KV_B, ng, _TP, N_KV_B, HD_B)[:, :, :, :t]
    yb = jnp.stack([o[:, 0, :, :, 0], o[:, 1, :, :, 1]], axis=1)
    yb = yb.transpose(3, 0, 1, 2, 4).reshape(t * b, N_HEADS_B * HD_B).astype(BF16)
    x1, qx = _mix_out([ya, yb], [wo_a, wo_b], to_tm(x2), g[1:2], g[2:3], wxq, tm=t * b)
    back = lambda a: to_bm(a).reshape(b * t, -1)
    state = (to_bm(k).reshape(b, t, N_KV_B, HD_B), to_bm(v).reshape(b, t, N_KV_B, HD_B),
             to_bm(ki), conv.reshape(2, b, D_CONV).transpose(1, 0, 2))
    return back(x1), back(qx), state


def kernel(x_prompt, x_sample, cache_k, cache_v, cache_idx_k, cache_mem_k, cache_mem_v, state_conv, state_C, state_n, state_m, page_table, mem_prompt, norm_g, w_in_even, conv_w, w_out_even, w_in_odd, b_i, b_f, hnorm_g, w_out_odd, mem_norm_g, w_xq, w_xk, w_xv, w_xo, w_ff1, w_ff2):
    bp, tp, _ = x_prompt.shape
    bs, ts, _ = x_sample.shape
    depth = norm_g.shape[0]
    tm_p = _tile(tp, 512)
    lc = _tile(tp, 256)
    nmem = mem_prompt.shape[1]
    mk_all, mv_all = _mem_kv(mem_prompt.reshape(bp * nmem, D_MODEL), mem_norm_g[:, None, :],
                             w_xk.astype(BF16), w_xv.astype(BF16), tm=_tile(bp * nmem, 512))
    xp = x_prompt.reshape(bp * tp, D_MODEL)
    xs = x_sample.reshape(bs * ts, D_MODEL)
    ev_p, ev_s, od_p, od_s = [], [], [], []
    for l in range(depth):
        g = norm_g[l]
        wxq = w_xq[l].astype(BF16)
        if l % 2 == 0:
            e = l // 2
            we = _prep_even(w_in_even[e], w_out_even[e])
            xp, qxp, st = _even_prompt(xp, g, we, conv_w[e], wxq, b=bp, t=tp, tm=tm_p)
            ev_p.append(st)
            xs, qxs, st = _even_sample(xs, g, we, conv_w[e], wxq, cache_idx_k[e], cache_k[e],
                                       cache_v[e], state_conv[e], page_table, b=bs, t=ts)
            ev_s.append(st)
        else:
            o = l // 2
            wod = _prep_odd(w_in_odd[o], b_i[o], b_f[o])
            hg = hnorm_g[o][None, :]
            wout = w_out_odd[o].astype(BF16)
            zc = jnp.zeros((bp, N_HEADS_C, DV_C, DQK_C), F32)
            zn = jnp.zeros((bp, N_HEADS_C, DQK_C), F32)
            zm = jnp.zeros((bp, N_HEADS_C), F32)
            xp, qxp, st = _odd_mixer(xp, g, wod, hg, wout, wxq, zc, zn, zm, b=bp, t=tp, tm=tm_p, lc=lc)
            od_p.append(st)
            xs, qxs, st = _odd_mixer(xs, g, wod, hg, wout, wxq, state_C[o], state_n[o], state_m[o],
                                     b=bs, t=ts, tm=bs * ts, lc=LANES)
            od_s.append(st)
        wxo = w_xo[l].astype(BF16)
        w1 = w_ff1[l].astype(BF16)
        w2 = w_ff2[l].astype(BF16)
        nx = N_HEADS_X * HD_X
        xp = _tail(xp, qxp, mk_all[l].reshape(bp, nmem, nx), mv_all[l].reshape(bp, nmem, nx),
                   wxo, g[3:4], g[4:5], g[5:6], w1, w2, b=bp, t=tp)
        xs = _tail(xs, qxs, cache_mem_k[l].reshape(bs, nmem, nx), cache_mem_v[l].reshape(bs, nmem, nx),
                   wxo, g[3:4], g[4:5], g[5:6], w1, w2, b=bs, t=ts)
    stack = lambda sts, j: jnp.stack([s[j] for s in sts])
    mem_shape = (depth, bp, nmem, N_HEADS_X, HD_X)
    return (xp.reshape(bp, tp, D_MODEL), xs.reshape(bs, ts, D_MODEL),
            stack(ev_p, 0), stack(ev_p, 1), stack(ev_p, 2), stack(ev_p, 3),
            stack(od_p, 0), stack(od_p, 1), stack(od_p, 2),
            mk_all.reshape(mem_shape), mv_all.reshape(mem_shape),
            stack(ev_s, 0), stack(ev_s, 1), stack(ev_s, 2), stack(ev_s, 3),
            stack(od_s, 0), stack(od_s, 1), stack(od_s, 2))
```

```python
import functools
import math

import jax
import jax.numpy as jnp
from jax import lax
from jax.experimental import pallas as pl
from jax.experimental.pallas import tpu as pltpu

F32 = jnp.float32
BF16 = jnp.bfloat16
I32 = jnp.int32

D_MODEL = 1024
D_CONV = 512
CONV_W = 3
N_HEADS_B = 8
N_KV_B = 2
HD_B = 64
N_IDX_HEADS = 8
IDX_DIM = 64
TOPK_MAX = 256
N_HEADS_C = 8
DQK_C = 64
DV_C = 128
N_MEM = 256
N_HEADS_X = 4
HD_X = 128
D_FF = 4096
PAGE_SIZE = 128
ROPE_THETA = 500000.0
NORM_EPS = 1e-6

LANES = 128
SUBLANES = 8
VMEM_LIMIT = 48 * 1024 * 1024

NEG_BIG = -1e30
INT_MIN = -(2 ** 31)
KEY_NEG_INF = -(2 ** 31) + 0x007FFFFF

_EV_U, _EV_GB, _EV_GC, _EV_Q, _EV_K, _EV_V, _EV_QI, _EV_KI, _EV_END = (
    0, 512, 1024, 1536, 2048, 2176, 2304, 2816, 2944)
_OD_Q, _OD_K, _OD_V, _OD_O, _OD_END = 0, 512, 1024, 2048, 3072


def _cparams(sem, vmem=VMEM_LIMIT):
    return pltpu.CompilerParams(dimension_semantics=sem, vmem_limit_bytes=vmem)


def _rms(x, g):
    return x * lax.rsqrt(jnp.mean(x * x, axis=-1, keepdims=True) + NORM_EPS) * g


def _dot(a, b):
    return jnp.dot(a, b, preferred_element_type=F32)


def _dot_nt(a, b):
    return lax.dot_general(a, b, (((1,), (1,)), ((), ())), preferred_element_type=F32)


def _rope128(x, c, sa, sb):
    return x * c + pltpu.roll(x, LANES - 8, 1) * sa + pltpu.roll(x, 8, 1) * sb


def _rope_tables(pos):
    r = HD_B // 4
    half = r // 2
    freqs = ROPE_THETA ** (-jnp.arange(half, dtype=F32) * 2.0 / r)
    ang = pos.astype(F32)[:, None] * freqs[None, :]
    cos, sin = jnp.cos(ang), jnp.sin(ang)
    t = pos.shape[0]
    ones = jnp.ones((t, HD_B - r), F32)
    zeros = jnp.zeros((t, HD_B - r), F32)
    zh = jnp.zeros((t, half), F32)
    c = jnp.concatenate([cos, cos, ones], axis=1)
    sa = jnp.concatenate([-sin, zh, zeros], axis=1)
    sb = jnp.concatenate([zh, sin, zeros], axis=1)
    tile2 = lambda a: jnp.concatenate([a, a], axis=1)
    return tile2(c), tile2(sa), tile2(sb)


_Q_SCALE = HD_B ** -0.5 * math.log2(math.e)
_ONES_ROWS = 16
_BOUND_SLACK = 1.01
_MIN_MASS = 2.0 ** -100


def _rope128_t(x, c, sa, sb):
    return x * c + pltpu.roll(x, LANES - 8, 0) * sa + pltpu.roll(x, 8, 0) * sb


def _even_proj_body(x_ref, g_ref, w_ref, wwi_ref, cw_ref, cos_ref, sa_ref, sb_ref, init_ref,
                    ya_ref, q_ref, k_ref, v_ref, qi_ref, ki_ref, wit_ref, conv_ref, *, time_major_b):
    tm = x_ref.shape[0]
    xn = _rms(x_ref[...], g_ref[...]).astype(BF16)
    c, sa, sb = cos_ref[...], sa_ref[...], sb_ref[...]

    def seg(a, b):
        return _dot(xn, w_ref[:, a:b])

    z = seg(_EV_GC, _EV_Q) * seg(_EV_U, _EV_GB)
    cw = cw_ref[...]
    nb = time_major_b
    init = init_ref[...]
    z1 = jnp.concatenate([init[nb:2 * nb], z[:tm - nb]], axis=0)
    z2 = jnp.concatenate([init, z[:tm - 2 * nb]], axis=0)
    conv_ref[...] = z[tm - 2 * nb:, :]
    conv = cw[0:1] * z2 + cw[1:2] * z1 + cw[2:3] * z
    ya_ref[...] = (seg(_EV_GB, _EV_GC) * conv).astype(BF16)

    for j in range(4):
        a = _EV_Q + j * LANES
        q_ref[:, j * LANES:(j + 1) * LANES] = (
            _rope128(seg(a, a + LANES), c, sa, sb) * _Q_SCALE).astype(BF16)
        a = _EV_QI + j * LANES
        qi_ref[:, j * LANES:(j + 1) * LANES] = (
            _rope128(seg(a, a + LANES), c, sa, sb) * IDX_DIM ** -0.5).astype(BF16)
    k_ref[...] = _rope128(seg(_EV_K, _EV_V), c, sa, sb)
    v_ref[...] = seg(_EV_V, _EV_QI)
    ki_ref[...] = _rope128(seg(_EV_KI, _EV_END), c, sa, sb)[:, :IDX_DIM]
    wit_ref[...] = _dot_nt(wwi_ref[...], xn) * N_IDX_HEADS ** -0.5


def _even_proj_prompt_body(x_ref, g_ref, w_ref, wt_ref, wwi_ref, cw_ref, cos_ref, sa_ref, sb_ref,
                           cost_ref, sat_ref, sbt_ref, init_ref,
                           ya_ref, k_ref, v_ref, ki_ref, kb_ref, kib_ref, qt_ref, qit_ref, vt_ref,
                           wit_ref, conv_ref, carry_ref, *, tiles_per_seq):
    tm = x_ref.shape[0]
    xn = _rms(x_ref[...], g_ref[...]).astype(BF16)
    c, sa, sb = cos_ref[...], sa_ref[...], sb_ref[...]
    ct, sat, sbt = cost_ref[...], sat_ref[...], sbt_ref[...]

    def seg(a, b):
        return _dot(xn, w_ref[:, a:b])

    z = seg(_EV_GC, _EV_Q) * seg(_EV_U, _EV_GB)
    cw = cw_ref[...]

    @pl.when(pl.program_id(0) % tiles_per_seq == 0)
    def _():
        carry_ref[0:2, :] = init_ref[0]

    c0 = carry_ref[0:1, :]
    c1 = carry_ref[1:2, :]
    row = lax.broadcasted_iota(I32, (tm, 1), 0)
    z1 = jnp.where(row == 0, c1, pltpu.roll(z, 1, 0))
    z2 = jnp.where(row == 0, c0, jnp.where(row == 1, c1, pltpu.roll(z, 2, 0)))
    carry_ref[0:2, :] = z[tm - 2:tm, :]
    conv_ref[0] = z[tm - 2:tm, :]
    conv = cw[0:1] * z2 + cw[1:2] * z1 + cw[2:3] * z
    ya_ref[...] = (seg(_EV_GB, _EV_GC) * conv).astype(BF16)

    k = _rope128(seg(_EV_K, _EV_V), c, sa, sb)
    k_ref[...] = k
    kb_ref[...] = k.astype(BF16)
    v_ref[...] = seg(_EV_V, _EV_QI)
    ki = _rope128(seg(_EV_KI, _EV_END), c, sa, sb)[:, :IDX_DIM]
    ki_ref[...] = ki
    kib_ref[...] = ki.astype(BF16)

    nq = N_HEADS_B * HD_B
    nqi = N_IDX_HEADS * IDX_DIM
    for j in range(nq // LANES):
        xt = _dot_nt(wt_ref[j * LANES:(j + 1) * LANES, :], xn)
        qt_ref[j * LANES:(j + 1) * LANES, :] = (_rope128_t(xt, ct, sat, sbt) * _Q_SCALE).astype(BF16)
    for j in range(nqi // LANES):
        xt = _dot_nt(wt_ref[nq + j * LANES:nq + (j + 1) * LANES, :], xn)
        qit_ref[j * LANES:(j + 1) * LANES, :] = (
            _rope128_t(xt, ct, sat, sbt) * IDX_DIM ** -0.5).astype(BF16)
    vt = _dot_nt(wt_ref[nq + nqi:, :], xn).astype(BF16)
    kc = vt_ref.shape[3]
    ones = jnp.ones((_ONES_ROWS, kc), BF16)
    for j in range(tm // kc):
        for g in range(N_KV_B):
            vt_ref[j, g] = jnp.concatenate(
                [vt[g * HD_B:(g + 1) * HD_B, j * kc:(j + 1) * kc], ones], axis=0)
    wit_ref[...] = _dot_nt(wwi_ref[...], xn) * N_IDX_HEADS ** -0.5


def _even_proj_prompt(x, g, w, wt, wwi, cw, tabs, tabs_t, init, *, tm, seq_len, kc):
    m = x.shape[0]
    tiles_per_seq = seq_len // tm
    row = lambda i: (i, 0)
    col = lambda i: (0, i)
    const = lambda i: (0, 0)
    seq = lambda i: (i // tiles_per_seq, 0, 0)
    nq = N_HEADS_B * HD_B
    nqi = N_IDX_HEADS * IDX_DIM
    nkv = N_KV_B * HD_B
    tab_spec = pl.BlockSpec((tm, LANES), lambda i: (i % tiles_per_seq, 0))
    tabt_spec = pl.BlockSpec((LANES, tm), lambda i: (0, i % tiles_per_seq))
    vrows = HD_B + _ONES_ROWS
    in_specs = [
        pl.BlockSpec((tm, D_MODEL), row), pl.BlockSpec((1, D_MODEL), const),
        pl.BlockSpec((D_MODEL, _EV_END), const), pl.BlockSpec(wt.shape, const),
        pl.BlockSpec((N_IDX_HEADS, D_MODEL), const), pl.BlockSpec((CONV_W, D_CONV), const),
        tab_spec, tab_spec, tab_spec, tabt_spec, tabt_spec, tabt_spec,
        pl.BlockSpec((1, 2, D_CONV), seq),
    ]
    out_specs = (
        pl.BlockSpec((tm, D_CONV), row),
        pl.BlockSpec((tm, nkv), row), pl.BlockSpec((tm, nkv), row), pl.BlockSpec((tm, IDX_DIM), row),
        pl.BlockSpec((tm, nkv), row), pl.BlockSpec((tm, IDX_DIM), row),
        pl.BlockSpec((nq, tm), col), pl.BlockSpec((nqi, tm), col),
        pl.BlockSpec((tm // kc, N_KV_B, vrows, kc), lambda i: (i, 0, 0, 0)),
        pl.BlockSpec((N_IDX_HEADS, tm), col),
        pl.BlockSpec((1, 2, D_CONV), seq),
    )
    out_shape = (
        jax.ShapeDtypeStruct((m, D_CONV), BF16),
        jax.ShapeDtypeStruct((m, nkv), F32), jax.ShapeDtypeStruct((m, nkv), F32),
        jax.ShapeDtypeStruct((m, IDX_DIM), F32),
        jax.ShapeDtypeStruct((m, nkv), BF16), jax.ShapeDtypeStruct((m, IDX_DIM), BF16),
        jax.ShapeDtypeStruct((nq, m), BF16), jax.ShapeDtypeStruct((nqi, m), BF16),
        jax.ShapeDtypeStruct((m // kc, N_KV_B, vrows, kc), BF16),
        jax.ShapeDtypeStruct((N_IDX_HEADS, m), F32),
        jax.ShapeDtypeStruct((m // seq_len, 2, D_CONV), F32),
    )
    return pl.pallas_call(
        functools.partial(_even_proj_prompt_body, tiles_per_seq=tiles_per_seq),
        grid=(m // tm,), in_specs=in_specs, out_specs=out_specs, out_shape=out_shape,
        scratch_shapes=[pltpu.VMEM((SUBLANES, D_CONV), F32)],
        compiler_params=_cparams(("arbitrary",)), name="even_proj_prompt",
    )(x, g, w, wt, wwi, cw, *tabs, *tabs_t, init)


def _even_proj(x, g, w, wwi, cw, tabs, init, *, seq_len, time_major_b):
    m = x.shape[0]
    tm = m
    nt = 1
    assert seq_len >= 2
    row = lambda i: (i, 0)
    const = lambda i: (0, 0)
    tab_spec = pl.BlockSpec((tm, LANES), const)
    init_spec = pl.BlockSpec((2 * time_major_b, D_CONV), const)
    conv_shape = jax.ShapeDtypeStruct((2 * time_major_b, D_CONV), F32)
    conv_spec = pl.BlockSpec((2 * time_major_b, D_CONV), const)
    scratch = []
    out_shape = (
        jax.ShapeDtypeStruct((m, D_CONV), BF16),
        jax.ShapeDtypeStruct((m, N_HEADS_B * HD_B), BF16),
        jax.ShapeDtypeStruct((m, N_KV_B * HD_B), F32),
        jax.ShapeDtypeStruct((m, N_KV_B * HD_B), F32),
        jax.ShapeDtypeStruct((m, N_IDX_HEADS * IDX_DIM), BF16),
        jax.ShapeDtypeStruct((m, IDX_DIM), F32),
        jax.ShapeDtypeStruct((N_IDX_HEADS, m), F32),
        conv_shape,
    )
    out_specs = (
        pl.BlockSpec((tm, D_CONV), row),
        pl.BlockSpec((tm, N_HEADS_B * HD_B), row),
        pl.BlockSpec((tm, N_KV_B * HD_B), row),
        pl.BlockSpec((tm, N_KV_B * HD_B), row),
        pl.BlockSpec((tm, N_IDX_HEADS * IDX_DIM), row),
        pl.BlockSpec((tm, IDX_DIM), row),
        pl.BlockSpec((N_IDX_HEADS, tm), lambda i: (0, i)),
        conv_spec,
    )
    in_specs = [
        pl.BlockSpec((tm, D_MODEL), row),
        pl.BlockSpec((1, D_MODEL), const),
        pl.BlockSpec((D_MODEL, _EV_END), const),
        pl.BlockSpec((N_IDX_HEADS, D_MODEL), const),
        pl.BlockSpec((CONV_W, D_CONV), const),
        tab_spec, tab_spec, tab_spec,
        init_spec,
    ]
    return pl.pallas_call(
        functools.partial(_even_proj_body, time_major_b=time_major_b),
        grid=(nt,), in_specs=in_specs, out_specs=out_specs, out_shape=out_shape,
        scratch_shapes=scratch, compiler_params=_cparams(("arbitrary",)),
        name="even_proj",
    )(x, g, w, wwi, cw, *tabs, init)


def _mix_out_body(*refs, n_in):
    ins = refs[:n_in]
    ws = refs[n_in:2 * n_in]
    x_ref, g1_ref, g2_ref, wq_ref, x1_ref, qx_ref = refs[2 * n_in:]
    y = _dot(ins[0][...], ws[0][...])
    for a, w in zip(ins[1:], ws[1:]):
        y = y + _dot(a[...], w[...])
    x1 = x_ref[...] + _rms(y, g1_ref[...])
    x1_ref[...] = x1
    xn = _rms(x1, g2_ref[...]).astype(BF16)
    qx_ref[...] = (_dot(xn, wq_ref[...]) * HD_X ** -0.5).astype(BF16)


def _mix_out(ins, ws, x, g1, g2, wq, *, tm):
    m = x.shape[0]
    row = lambda i: (i, 0)
    const = lambda i: (0, 0)
    n_in = len(ins)
    in_specs = ([pl.BlockSpec((tm, a.shape[1]), row) for a in ins]
                + [pl.BlockSpec(w.shape, const) for w in ws]
                + [pl.BlockSpec((tm, D_MODEL), row), pl.BlockSpec((1, D_MODEL), const),
                   pl.BlockSpec((1, D_MODEL), const), pl.BlockSpec(wq.shape, const)])
    nq = wq.shape[1]
    return pl.pallas_call(
        functools.partial(_mix_out_body, n_in=n_in),
        grid=(m // tm,), in_specs=in_specs,
        out_specs=(pl.BlockSpec((tm, D_MODEL), row), pl.BlockSpec((tm, nq), row)),
        out_shape=(jax.ShapeDtypeStruct((m, D_MODEL), F32), jax.ShapeDtypeStruct((m, nq), BF16)),
        compiler_params=_cparams(("parallel",)), name="mix_out",
    )(*ins, *ws, x, g1, g2, wq)


def _mem_kv_body(mem_ref, g_ref, wk_ref, wv_ref, mk_ref, mv_ref):
    mn = _rms(mem_ref[...], g_ref[0]).astype(BF16)
    mk_ref[0] = _dot(mn, wk_ref[0])
    mv_ref[0] = _dot(mn, wv_ref[0])


def _mem_kv(mem, g, wk, wv, *, tm):
    m = mem.shape[0]
    depth = g.shape[0]
    n = wk.shape[2]
    wspec = pl.BlockSpec((1, D_MODEL, n), lambda l, i: (l, 0, 0))
    ospec = pl.BlockSpec((1, tm, n), lambda l, i: (l, i, 0))
    oshape = jax.ShapeDtypeStruct((depth, m, n), F32)
    return pl.pallas_call(
        _mem_kv_body, grid=(depth, m // tm),
        in_specs=[pl.BlockSpec((tm, D_MODEL), lambda l, i: (i, 0)),
                  pl.BlockSpec((1, 1, D_MODEL), lambda l, i: (l, 0, 0)), wspec, wspec],
        out_specs=(ospec, ospec), out_shape=(oshape, oshape),
        compiler_params=_cparams(("parallel", "parallel")), name="mem_kv",
    )(mem, g, wk, wv)


def _xattn_body(q_ref, mk_ref, mv_ref, x_ref, wo_ref, g_ref, o_ref):
    q = q_ref[0]
    mk = mk_ref[0].astype(BF16)
    mv = mv_ref[0].astype(BF16)
    outs = []
    for h in range(N_HEADS_X):
        sl = slice(h * HD_X, (h + 1) * HD_X)
        s = _dot_nt(q[:, sl], mk[:, sl])
        s = s - jnp.max(s, axis=-1, keepdims=True)
        p = jnp.exp(s)
        p = p / jnp.sum(p, axis=-1, keepdims=True)
        outs.append(_dot(p.astype(BF16), mv[:, sl]))
    o = jnp.concatenate(outs, axis=-1).astype(BF16)
    o_ref[0] = x_ref[0] + _rms(_dot(o, wo_ref[...]), g_ref[...])


def _xattn(q, mk, mv, x, wo, g, *, tm):
    b, t, _ = x.shape
    nq = q.shape[2]
    tile = lambda i, j: (i, j, 0)
    per_b = lambda i, j: (i, 0, 0)
    const = lambda i, j: (0, 0)
    return pl.pallas_call(
        _xattn_body, grid=(b, t // tm),
        in_specs=[pl.BlockSpec((1, tm, nq), tile),
                  pl.BlockSpec((1, N_MEM, nq), per_b), pl.BlockSpec((1, N_MEM, nq), per_b),
                  pl.BlockSpec((1, tm, D_MODEL), tile),
                  pl.BlockSpec(wo.shape, const), pl.BlockSpec((1, D_MODEL), const)],
        out_specs=pl.BlockSpec((1, tm, D_MODEL), tile),
        out_shape=jax.ShapeDtypeStruct((b, t, D_MODEL), F32),
        compiler_params=_cparams(("parallel", "parallel")), name="xattn",
    )(q, mk, mv, x, wo, g)


def _mlp_body(x_ref, g4_ref, g5_ref, w1_ref, w2_ref, o_ref, xn_ref, acc_ref):
    k = pl.program_id(1)

    @pl.when(k == 0)
    def _():
        xn_ref[...] = _rms(x_ref[...], g4_ref[...]).astype(BF16)

    h = jnp.maximum(_dot(xn_ref[...], w1_ref[...]), 0.0)
    part = _dot((h * h).astype(BF16), w2_ref[...])

    @pl.when(k == 0)
    def _():
        acc_ref[...] = part

    @pl.when(k > 0)
    def _():
        acc_ref[...] += part

    @pl.when(k == pl.num_programs(1) - 1)
    def _():
        o_ref[...] = x_ref[...] + _rms(acc_ref[...], g5_ref[...])


def _mlp(x, g4, g5, w1, w2, *, tm, tf):
    m = x.shape[0]
    f = w1.shape[1]
    row = lambda i, k: (i, 0)
    const = lambda i, k: (0, 0)
    return pl.pallas_call(
        _mlp_body, grid=(m // tm, f // tf),
        in_specs=[pl.BlockSpec((tm, D_MODEL), row), pl.BlockSpec((1, D_MODEL), const),
                  pl.BlockSpec((1, D_MODEL), const),
                  pl.BlockSpec((D_MODEL, tf), lambda i, k: (0, k)),
                  pl.BlockSpec((tf, D_MODEL), lambda i, k: (k, 0))],
        out_specs=pl.BlockSpec((tm, D_MODEL), row),
        out_shape=jax.ShapeDtypeStruct((m, D_MODEL), F32),
        scratch_shapes=[pltpu.VMEM((tm, D_MODEL), BF16), pltpu.VMEM((tm, D_MODEL), F32)],
        compiler_params=_cparams(("parallel", "arbitrary")), name="mlp",
    )(x, g4, g5, w1, w2)


def _for_pairs(n, body):
    def pair(j, carry):
        body(2 * j, carry)
        body(2 * j + 1, carry)
        return carry

    lax.fori_loop(0, n // 2, pair, 0)

    @pl.when(n % 2 == 1)
    def _():
        body(n - 1, 0)


def _sortable_key(score):
    bits = pltpu.bitcast(score, I32)
    return bits ^ ((bits >> 31) & 0x7FFFFFFF)


def _dsa_prompt_body(qit_ref, wit_ref, ki_ref, qt_ref, k_ref, vt_ref, o_ref,
                     key_ref, qpad_ref, m_ref, acc_ref, knorm_ref, *, ktop, idx_bits):
    tq = qit_ref.shape[1]
    kc = tq
    i = pl.program_id(1)
    nck = i + 1
    kiota = lax.broadcasted_iota(I32, (kc, tq), 0)
    qidx = i * tq + lax.broadcasted_iota(I32, (kc, tq), 1)
    ng = N_HEADS_B // N_KV_B

    def chunk_off(c):
        return pl.multiple_of(c * kc, kc)

    def score_chunk(c, carry):
        off = chunk_off(c)
        kic = ki_ref[pl.ds(off, kc), :]
        sc = jnp.zeros((kc, tq), F32)
        for h in range(N_IDX_HEADS):
            s = _dot(kic, qit_ref[h * IDX_DIM:(h + 1) * IDX_DIM, :])
            sc = sc + wit_ref[h:h + 1, :] * jnp.maximum(s, 0.0)
        sc = jnp.where(kiota + off <= qidx, sc, -jnp.inf)
        key_ref[pl.ds(off, kc), :] = _sortable_key(sc)
        return carry

    _for_pairs(nck, score_chunk)

    def count(pred):
        def body(c, acc):
            off = chunk_off(c)
            hit = pred(key_ref[pl.ds(off, kc), :], kiota + off).astype(I32)
            return acc + jnp.sum(hit.reshape(kc // SUBLANES, SUBLANES, tq), axis=0)

        acc = lax.fori_loop(0, nck, body, jnp.zeros((SUBLANES, tq), I32))
        return jnp.sum(acc, axis=0, keepdims=True)

    def bit_step(j, lo):
        cand = lo + jnp.left_shift(jnp.int32(1), 31 - j)
        cnt = count(lambda blk, _: blk >= cand)
        return jnp.where(cnt >= ktop, cand, lo)

    thr = lax.fori_loop(0, 32, bit_step, jnp.full((1, tq), INT_MIN, I32))

    cnt_gt = count(lambda blk, _: blk > thr)
    cnt_ge = count(lambda blk, _: blk >= thr)
    tie = jnp.logical_and(thr > KEY_NEG_INF, cnt_ge > ktop)

    @pl.when(jnp.max(tie.astype(I32)) > 0)
    def _():
        need = ktop - cnt_gt

        def idx_step(j, res):
            cand = res + jnp.left_shift(jnp.int32(1), idx_bits - 1 - j)
            cnt = count(lambda blk, kidx: jnp.logical_and(blk == thr, kidx < cand))
            return jnp.where(cnt < need, cand, res)

        jcut = lax.fori_loop(0, idx_bits, idx_step, jnp.zeros((1, tq), I32))

        def drop_chunk(c, carry):
            off = chunk_off(c)
            blk = key_ref[pl.ds(off, kc), :]
            drop = jnp.logical_and(blk == thr, kiota + off > jcut)
            key_ref[pl.ds(off, kc), :] = jnp.where(drop, blk - 1, blk)
            return carry

        lax.fori_loop(0, nck, drop_chunk, 0)

    thr_eff = jnp.maximum(thr, KEY_NEG_INF + 1)

    @pl.when(i == 0)
    def _():
        def knorm_chunk(c, mx):
            kk = k_ref[pl.ds(chunk_off(c), kc), :].astype(F32)
            sq = kk * kk
            per_g = [jnp.max(jnp.sum(sq[:, g * HD_B:(g + 1) * HD_B], axis=1, keepdims=True),
                             axis=0, keepdims=True) for g in range(N_KV_B)]
            return jnp.maximum(mx, jnp.concatenate(per_g, axis=0))

        mx = lax.fori_loop(0, k_ref.shape[0] // kc, knorm_chunk, jnp.zeros((N_KV_B, 1), F32))
        knorm_ref[...] = jnp.broadcast_to(mx, knorm_ref.shape)

    zero_half = jnp.zeros((HD_B, tq), BF16)
    for g in range(N_KV_B):
        cols, bounds = [], []
        for hh in range(ng):
            h = g * ng + hh
            qh = qt_ref[h * HD_B:(h + 1) * HD_B, :]
            halves = [zero_half] * N_KV_B
            halves[g] = qh
            cols.append(jnp.concatenate(halves, axis=0))
            qf = qh.astype(F32)
            qn2 = jnp.sum(qf * qf, axis=0, keepdims=True)
            bounds.append(jnp.sqrt(qn2 * knorm_ref[g:g + 1, 0:1]) * _BOUND_SLACK)
        qpad_ref[g, 0:N_KV_B * HD_B, :] = jnp.concatenate(cols, axis=1)
        neg_bound = jnp.broadcast_to(-jnp.concatenate(bounds, axis=1), (2 * SUBLANES, ng * tq))
        qpad_ref[g, N_KV_B * HD_B:N_KV_B * HD_B + 2 * SUBLANES, :] = jnp.where(
            lax.broadcasted_iota(I32, neg_bound.shape, 0) == 0, neg_bound, 0.0).astype(BF16)
    acc_ref[...] = jnp.zeros(acc_ref.shape, F32)
    ones_cols = jnp.ones((kc, 2 * SUBLANES), BF16)

    def chunk_inputs(c):
        off = chunk_off(c)
        bias = jnp.where(key_ref[pl.ds(off, kc), :] >= thr_eff, 0.0, NEG_BIG)
        return jnp.concatenate([bias] * ng, axis=1), k_ref[pl.ds(off, kc), :]

    def attn_chunk(c, carry):
        bias, kch = chunk_inputs(c)
        kaug = jnp.concatenate([kch, ones_cols], axis=1)
        for g in range(N_KV_B):
            p = jnp.exp2(_dot(kaug, qpad_ref[g]) + bias).astype(BF16)
            acc_ref[g] += _dot(vt_ref[c, g], p)
        return carry

    _for_pairs(nck, attn_chunk)

    @pl.when(jnp.min(acc_ref[:, HD_B:HD_B + 1, :]) < _MIN_MASS)
    def _():
        m_ref[...] = jnp.full(m_ref.shape, NEG_BIG, F32)
        acc_ref[...] = jnp.zeros(acc_ref.shape, F32)

        def exact_chunk(c, carry):
            bias, kch = chunk_inputs(c)
            for g in range(N_KV_B):
                s = _dot(kch, qpad_ref[g, 0:N_KV_B * HD_B, :]) + bias
                m_old = m_ref[g]
                m_new = jnp.maximum(m_old, jnp.max(s, axis=0, keepdims=True))
                alpha = jnp.exp2(m_old - m_new)
                p = jnp.exp2(s - m_new).astype(BF16)
                acc_ref[g] = alpha * acc_ref[g] + _dot(vt_ref[c, g], p)
                m_ref[g] = m_new
            return carry

        lax.fori_loop(0, nck, exact_chunk, 0)

    for g in range(N_KV_B):
        acc = acc_ref[g]
        o = acc[:HD_B] / acc[HD_B:HD_B + 1]
        for hp in range(ng // 2):
            pair = jnp.concatenate([o[:, (2 * hp) * tq:(2 * hp + 1) * tq],
                                    o[:, (2 * hp + 1) * tq:(2 * hp + 2) * tq]], axis=0)
            lane0 = (g * ng + 2 * hp) * HD_B
            o_ref[:, lane0:lane0 + 2 * HD_B] = pair.T.astype(BF16)


def _dsa_prompt(qit, wit, kib, qt, kb, vt, *, b, t, tq):
    hd = qt.shape[0]
    nt = t // tq
    ktop = min(TOPK_MAX, t // 4)
    idx_bits = max(1, (t - 1).bit_length())
    ng = N_HEADS_B // N_KV_B
    vrows = vt.shape[2]
    col = lambda bi, i: (0, bi * nt + i)
    return pl.pallas_call(
        functools.partial(_dsa_prompt_body, ktop=ktop, idx_bits=idx_bits),
        grid=(b, nt),
        in_specs=[
            pl.BlockSpec((hd, tq), col),
            pl.BlockSpec((N_IDX_HEADS, tq), col),
            pl.BlockSpec((t, IDX_DIM), lambda bi, i: (bi, 0)),
            pl.BlockSpec((hd, tq), col),
            pl.BlockSpec((t, N_KV_B * HD_B), lambda bi, i: (bi, 0)),
            pl.BlockSpec((nt, N_KV_B, vrows, tq), lambda bi, i: (bi, 0, 0, 0)),
        ],
        out_specs=pl.BlockSpec((tq, hd), lambda bi, i: (bi * nt + i, 0)),
        out_shape=jax.ShapeDtypeStruct((b * t, hd), BF16),
        scratch_shapes=[pltpu.VMEM((t, tq), I32),
                        pltpu.VMEM((N_KV_B, N_KV_B * HD_B + 2 * SUBLANES, ng * tq), BF16),
                        pltpu.VMEM((N_KV_B, 1, ng * tq), F32),
                        pltpu.VMEM((N_KV_B, vrows, ng * tq), F32),
                        pltpu.VMEM((N_KV_B, LANES), F32)],
        compiler_params=_cparams(("parallel", "arbitrary")), name="dsa_prompt",
    )(qit, wit, kib, qt, kb, vt)


_TP = SUBLANES
_NB = LANES
_PG_MAX = 16


def _dsa_sample_body(pt_ref, qi_ref, w_ref, kin_ref, q_ref, kn_ref, vn_ref,
                     ci_hbm, ck_hbm, cv_hbm, o_ref,
                     ibuf, kbuf, vbuf, key_ref, keyn_ref, isem, ksem, vsem,
                     *, n_pages, pg, t_new, ktop, idx_bits):
    b = pl.program_id(0)
    slot = b % 2
    ck = pg * PAGE_SIZE
    nch = n_pages // pg
    past = n_pages * PAGE_SIZE

    def idx_copy(sl, p, page):
        return pltpu.make_async_copy(ci_hbm.at[page], ibuf.at[sl, p], isem.at[sl])

    def k_copy(p, page):
        return pltpu.make_async_copy(ck_hbm.at[page], kbuf.at[p], ksem)

    def v_copy(p, page):
        return pltpu.make_async_copy(cv_hbm.at[page], vbuf.at[p], vsem)

    def issue_idx(bb, sl):
        def f(p, carry):
            idx_copy(sl, p, pt_ref[bb, p]).start()
            return carry

        lax.fori_loop(0, n_pages, f, 0)

    @pl.when(b == 0)
    def _():
        issue_idx(0, 0)

    def issue_kv(p, carry):
        page = pt_ref[b, p]
        k_copy(p, page).start()
        v_copy(p, page).start()
        return carry

    lax.fori_loop(0, n_pages, issue_kv, 0)

    @pl.when(b + 1 < pl.num_programs(0))
    def _():
        issue_idx(b + 1, 1 - slot)

    def wait_pages(make):
        def w(p, carry):
            make(p).wait()
            return carry

        lax.fori_loop(0, n_pages, w, 0)

    lane = lax.broadcasted_iota(I32, (_TP, ck), 1)
    lane_n = lax.broadcasted_iota(I32, (_TP, _NB), 1)
    row_n = lax.broadcasted_iota(I32, (_TP, _NB), 0)

    qi = qi_ref[0]
    nt = qi.shape[0] // N_IDX_HEADS

    def scores(kpt):
        n = kpt.shape[1]
        s = jnp.maximum(_dot(qi, kpt), 0.0) * w_ref[0]
        return jnp.sum(s.reshape(N_IDX_HEADS, nt, n), axis=0)

    def pages_t(buf, first):
        return jnp.concatenate([buf[first + p] for p in range(pg)], axis=1).astype(BF16)

    wait_pages(lambda p: idx_copy(slot, p, 0))

    def sc_chunk(c, carry):
        key_ref[c] = _sortable_key(scores(pages_t(ibuf.at[slot], c * pg)))
        return carry

    lax.fori_loop(0, nch, sc_chunk, 0)
    admissible = jnp.logical_and(lane_n <= row_n, row_n < t_new)
    keyn_ref[...] = _sortable_key(jnp.where(admissible, scores(kin_ref[0]), -jnp.inf))

    def count(pred):
        def body(c, acc):
            return acc + pred(key_ref[c], lane + c * ck).astype(I32)

        acc = lax.fori_loop(0, nch, body, jnp.zeros((_TP, ck), I32))
        acc_n = pred(keyn_ref[...], lane_n + past).astype(I32)
        return jnp.sum(acc, axis=1, keepdims=True) + jnp.sum(acc_n, axis=1, keepdims=True)

    def bit_step(j, lo):
        cand = lo + jnp.left_shift(jnp.int32(1), 31 - j)
        cnt = count(lambda blk, _: blk >= cand)
        return jnp.where(cnt >= ktop, cand, lo)

    thr = lax.fori_loop(0, 32, bit_step, jnp.full((_TP, 1), INT_MIN, I32))
    cnt_gt = count(lambda blk, _: blk > thr)
    cnt_ge = count(lambda blk, _: blk >= thr)
    tie = jnp.logical_and(thr > KEY_NEG_INF, cnt_ge > ktop)

    @pl.when(jnp.max(tie.astype(I32)) > 0)
    def _():
        need = ktop - cnt_gt

        def idx_step(j, res):
            cand = res + jnp.left_shift(jnp.int32(1), idx_bits - 1 - j)
            cnt = count(lambda blk, kidx: jnp.logical_and(blk == thr, kidx < cand))
            return jnp.where(cnt < need, cand, res)

        jcut = lax.fori_loop(0, idx_bits, idx_step, jnp.zeros((_TP, 1), I32))

        def dropped(blk, kidx):
            return jnp.where(jnp.logical_and(blk == thr, kidx > jcut), blk - 1, blk)

        def drop_chunk(c, carry):
            key_ref[c] = dropped(key_ref[c], lane + c * ck)
            return carry

        lax.fori_loop(0, nch, drop_chunk, 0)
        keyn_ref[...] = dropped(keyn_ref[...], lane_n + past)

    thr_eff = jnp.maximum(thr, KEY_NEG_INF + 1)
    qp = q_ref[0]
    reps = qp.shape[0] // _TP

    def attend(keys, kct, vct, carry):
        m, l, acc = carry
        bias = jnp.where(keys >= thr_eff, 0.0, NEG_BIG)
        s = _dot(qp, kct) + jnp.concatenate([bias] * reps, axis=0)
        m_new = jnp.maximum(m, jnp.max(s, axis=-1, keepdims=True))
        alpha = jnp.exp2(m - m_new)
        p = jnp.exp2(s - m_new)
        l = alpha * l + jnp.sum(p, axis=-1, keepdims=True)
        acc = alpha * acc + _dot_nt(p.astype(BF16), vct)
        return m_new, l, acc

    wait_pages(lambda p: k_copy(p, 0))
    wait_pages(lambda p: v_copy(p, 0))

    def at_chunk(c, carry):
        return attend(key_ref[c], pages_t(kbuf, c * pg), pages_t(vbuf, c * pg), carry)

    nr = qp.shape[0]
    init = (jnp.full((nr, 1), NEG_BIG, F32), jnp.zeros((nr, 1), F32),
            jnp.zeros((nr, N_KV_B * HD_B), F32))
    carry = lax.fori_loop(0, nch, at_chunk, init)
    m, l, acc = attend(keyn_ref[...], kn_ref[0], vn_ref[0], carry)
    o_ref[0] = acc / l


def _dsa_sample(page_table, qi, w, kin, qp, kn, vn, cache_i, cache_k, cache_v, *, t_new):
    b, n_pages = page_table.shape
    pg = math.gcd(n_pages, _PG_MAX)
    nr = qp.shape[1]
    total = n_pages * PAGE_SIZE + t_new
    ktop = min(TOPK_MAX, total // 4)
    idx_bits = max(1, (total - 1).bit_length())
    nch = n_pages // pg
    per_b = lambda i, pt: (i, 0, 0)
    grid_spec = pltpu.PrefetchScalarGridSpec(
        num_scalar_prefetch=1, grid=(b,),
        in_specs=[pl.BlockSpec((1,) + qi.shape[1:], per_b), pl.BlockSpec((1,) + w.shape[1:], per_b),
                  pl.BlockSpec((1,) + kin.shape[1:], per_b), pl.BlockSpec((1,) + qp.shape[1:], per_b),
                  pl.BlockSpec((1,) + kn.shape[1:], per_b), pl.BlockSpec((1,) + vn.shape[1:], per_b),
                  pl.BlockSpec(memory_space=pl.ANY), pl.BlockSpec(memory_space=pl.ANY),
                  pl.BlockSpec(memory_space=pl.ANY)],
        out_specs=pl.BlockSpec((1, nr, N_KV_B * HD_B), per_b),
        scratch_shapes=[pltpu.VMEM((2, n_pages, IDX_DIM, PAGE_SIZE), F32),
                        pltpu.VMEM((n_pages, N_KV_B * HD_B, PAGE_SIZE), F32),
                        pltpu.VMEM((n_pages, N_KV_B * HD_B, PAGE_SIZE), F32),
                        pltpu.VMEM((nch, _TP, pg * PAGE_SIZE), I32),
                        pltpu.VMEM((_TP, _NB), I32),
                        pltpu.SemaphoreType.DMA((2,)),
                        pltpu.SemaphoreType.DMA(()),
                        pltpu.SemaphoreType.DMA(())])
    return pl.pallas_call(
        functools.partial(_dsa_sample_body, n_pages=n_pages, pg=pg, t_new=t_new, ktop=ktop,
                          idx_bits=idx_bits),
        grid_spec=grid_spec,
        out_shape=jax.ShapeDtypeStruct((b, nr, N_KV_B * HD_B), F32),
        compiler_params=_cparams(("arbitrary",)), name="dsa_sample",
    )(page_table, qi, w, kin, qp, kn, vn, cache_i, cache_k, cache_v)


def _log_sigmoid(x):
    return jnp.minimum(x, 0.0) - jnp.log1p(jnp.exp(-jnp.abs(x)))


def _odd_proj_body(x_ref, g_ref, w_ref, wt_ref, wg_ref, wgt_ref, bg_ref, bgt_ref,
                   qt_ref, k_ref, vt_ref, so_ref, gate_ref, gatet_ref):
    xn = _rms(x_ref[...], g_ref[...]).astype(BF16)
    nqk = N_HEADS_C * DQK_C
    k_ref[...] = _dot(xn, w_ref[:, :nqk]).astype(BF16)
    so_ref[...] = jax.nn.sigmoid(_dot(xn, w_ref[:, nqk:])).astype(BF16)
    qt_ref[...] = (_dot_nt(wt_ref[:nqk, :], xn) * DQK_C ** -0.5).astype(BF16)
    vt_ref[...] = _dot_nt(wt_ref[nqk:, :], xn).astype(BF16)
    nh = N_HEADS_C
    gate = _dot(xn, wg_ref[...]) + bg_ref[...]
    col = lax.broadcasted_iota(I32, gate.shape, 1)
    gate_ref[...] = jnp.where(col < nh, gate, _log_sigmoid(gate))
    gatet = _dot_nt(wgt_ref[...], xn) + bgt_ref[...]
    rowi = lax.broadcasted_iota(I32, gatet.shape, 0)
    gatet_ref[...] = jnp.where(rowi < nh, gatet, _log_sigmoid(gatet))


def _odd_proj(x, g, w, wt, wg, wgt, bg, bgt, *, tm):
    m = x.shape[0]
    row = lambda i: (i, 0)
    col = lambda i: (0, i)
    const = lambda i: (0, 0)
    nqk = N_HEADS_C * DQK_C
    nv = N_HEADS_C * DV_C
    ng = 2 * N_HEADS_C
    return pl.pallas_call(
        _odd_proj_body, grid=(m // tm,),
        in_specs=[pl.BlockSpec((tm, D_MODEL), row), pl.BlockSpec((1, D_MODEL), const),
                  pl.BlockSpec(w.shape, const), pl.BlockSpec(wt.shape, const),
                  pl.BlockSpec((D_MODEL, ng), const),
                  pl.BlockSpec((ng, D_MODEL), const), pl.BlockSpec((1, ng), const),
                  pl.BlockSpec((ng, 1), const)],
        out_specs=(pl.BlockSpec((nqk, tm), col), pl.BlockSpec((tm, nqk), row),
                   pl.BlockSpec((nv, tm), col), pl.BlockSpec((tm, nv), row),
                   pl.BlockSpec((tm, ng), row), pl.BlockSpec((ng, tm), col)),
        out_shape=(jax.ShapeDtypeStruct((nqk, m), BF16), jax.ShapeDtypeStruct((m, nqk), BF16),
                   jax.ShapeDtypeStruct((nv, m), BF16), jax.ShapeDtypeStruct((m, nv), BF16),
                   jax.ShapeDtypeStruct((m, ng), F32), jax.ShapeDtypeStruct((ng, m), F32)),
        compiler_params=_cparams(("parallel",)), name="odd_proj",
    )(x, g, w, wt, wg, wgt, bg, bgt)


def _split3(x):
    hi = x.astype(BF16)
    r1 = x - hi.astype(F32)
    mid = r1.astype(BF16)
    lo = (r1 - mid.astype(F32)).astype(BF16)
    return hi, mid, lo


def _mlstm_body(qt_ref, k_ref, vt_ref, so_ref, gate_ref, gatet_ref, hg_ref,
                c0_ref, n0_ref, m0_ref,
                h_ref, c_ref, n_ref, m_ref, cs_ref, ns_ref, ms_ref):
    lc = k_ref.shape[0]
    nh = N_HEADS_C
    c = pl.program_id(1)

    @pl.when(c == 0)
    def _():
        cs_ref[...] = c0_ref[0]
        ns_ref[...] = n0_ref[0]
        ms_ref[...] = jnp.broadcast_to(m0_ref[0], ms_ref.shape)

    ri = lax.broadcasted_iota(I32, (lc, lc), 0)
    ci = lax.broadcasted_iota(I32, (lc, lc), 1)
    causal_t = ri <= ci
    tril = (ci <= ri).astype(BF16)
    triu = causal_t.astype(BF16)
    gate = gate_ref[...]
    gatet = gatet_ref[...]
    lf3 = _split3(gate[:, nh:])
    bcols = _dot(tril, lf3[0]) + _dot(tril, lf3[1]) + _dot(tril, lf3[2])
    lft3 = _split3(gatet[nh:, :])
    brows = _dot(lft3[0], triu) + _dot(lft3[1], triu) + _dot(lft3[2], triu)
    half_lane = lax.broadcasted_iota(I32, (1, 2 * DQK_C), 1) // DQK_C
    zero_q = jnp.zeros((DQK_C, lc), BF16)
    ms = ms_ref[...]
    ms_new = []
    for h in range(nh):
        j, half = divmod(h, 2)
        in_half = half_lane == half
        kp = k_ref[:, j * 2 * DQK_C:(j + 1) * 2 * DQK_C]
        qh = qt_ref[h * DQK_C:(h + 1) * DQK_C, :]
        qpad = jnp.concatenate([qh, zero_q] if half == 0 else [zero_q, qh], axis=0)
        vt = vt_ref[h * DV_C:(h + 1) * DV_C, :]
        br = brows[h:h + 1, :]
        igr = gatet[h:h + 1, :]
        m_prev = ms[h:h + 1, 0:1]
        a = br + m_prev
        src = gate[:, h:h + 1] - bcols[:, h:h + 1]
        d = jnp.where(causal_t, br + src, NEG_BIG)
        mj = jnp.maximum(a, jnp.max(d, axis=0, keepdims=True))
        s = _dot(kp, qpad) * jnp.exp(d - mj)
        aw = jnp.exp(a - mj)
        cp = cs_ref[j]
        n8 = jnp.broadcast_to(ns_ref[j:j + 1, :], (SUBLANES, 2 * DQK_C)).astype(BF16)
        num = _dot(vt, s.astype(BF16)) + aw * _dot(cp.astype(BF16), qpad)
        den = jnp.sum(s, axis=0, keepdims=True) + aw * _dot(n8, qpad)[0:1]
        ht = num / jnp.maximum(jnp.abs(den), jnp.exp(-mj))
        ht = ht * lax.rsqrt(jnp.mean(ht * ht, axis=0, keepdims=True) + NORM_EPS)
        sl = slice(h * DV_C, (h + 1) * DV_C)
        h_ref[:, sl] = (ht.T * hg_ref[:, sl] * so_ref[:, sl].astype(F32)).astype(BF16)
        b_last = br[:, lc - 1:lc]
        g_row = b_last - br + igr
        m_new = jnp.maximum(b_last + m_prev, jnp.max(g_row, axis=-1, keepdims=True))
        gw = jnp.exp(g_row - m_new)
        decay = jnp.exp(b_last + m_prev - m_new)
        upd = _dot((vt.astype(F32) * gw).astype(BF16), kp)
        cs_ref[j] = jnp.where(in_half, decay * cp + upd, cp)
        gw8 = jnp.broadcast_to(gw, (SUBLANES, lc)).astype(BF16)
        n_old = ns_ref[j:j + 1, :]
        ns_ref[j:j + 1, :] = jnp.where(in_half, decay * n_old + _dot(gw8, kp)[0:1], n_old)
        ms_new.append(jnp.broadcast_to(m_new, (1, ms.shape[1])))
    ms_ref[...] = jnp.concatenate(ms_new, axis=0)

    @pl.when(c == pl.num_programs(1) - 1)
    def _():
        c_ref[0] = cs_ref[...]
        n_ref[0] = ns_ref[...]
        m_ref[0] = ms_ref[...]


def _mlstm(qt, k, vt, so, gate, gatet, hg, c0p, n0p, m0, *, b, t, lc):
    nh = N_HEADS_C
    nqk = nh * DQK_C
    nv = nh * DV_C
    ng = gate.shape[1]
    nc = t // lc
    npair = nh // 2
    row = lambda bi, c: (bi * nc + c, 0)
    col = lambda bi, c: (0, bi * nc + c)
    per_b4 = lambda bi, c: (bi, 0, 0, 0)
    per_b3 = lambda bi, c: (bi, 0, 0)
    return pl.pallas_call(
        _mlstm_body, grid=(b, nc),
        in_specs=[pl.BlockSpec((nqk, lc), col), pl.BlockSpec((lc, nqk), row),
                  pl.BlockSpec((nv, lc), col), pl.BlockSpec((lc, nv), row),
                  pl.BlockSpec((lc, ng), row), pl.BlockSpec((ng, lc), col),
                  pl.BlockSpec((1, nv), lambda bi, c: (0, 0)),
                  pl.BlockSpec((1, npair, DV_C, 2 * DQK_C), per_b4),
                  pl.BlockSpec((1, npair, 2 * DQK_C), per_b3),
                  pl.BlockSpec((1, nh, 1), per_b3)],
        out_specs=(pl.BlockSpec((lc, nv), row),
                   pl.BlockSpec((1, npair, DV_C, 2 * DQK_C), per_b4),
                   pl.BlockSpec((1, npair, 2 * DQK_C), per_b3),
                   pl.BlockSpec((1, nh, LANES), per_b3)),
        out_shape=(jax.ShapeDtypeStruct((b * t, nv), BF16),
                   jax.ShapeDtypeStruct((b, npair, DV_C, 2 * DQK_C), F32),
                   jax.ShapeDtypeStruct((b, npair, 2 * DQK_C), F32),
                   jax.ShapeDtypeStruct((b, nh, LANES), F32)),
        scratch_shapes=[pltpu.VMEM((npair, DV_C, 2 * DQK_C), F32),
                        pltpu.VMEM((npair, 2 * DQK_C), F32), pltpu.VMEM((nh, LANES), F32)],
        compiler_params=_cparams(("parallel", "arbitrary")), name="mlstm",
    )(qt, k, vt, so, gate, gatet, hg, c0p, n0p, m0)


def _prep_even(w_in, w_out):
    sizes = [D_CONV, D_CONV, D_CONV, N_HEADS_B * HD_B, N_KV_B * HD_B, N_KV_B * HD_B,
             N_IDX_HEADS * IDX_DIM, N_IDX_HEADS]
    offs = [sum(sizes[:j + 1]) for j in range(len(sizes))]
    u, gb, gc, q, k, v, qi, wi, ki = jnp.split(w_in, offs, axis=1)
    pad = jnp.zeros((D_MODEL, LANES - IDX_DIM), w_in.dtype)
    w = jnp.concatenate([u, gb, gc, q, k, v, qi, ki, pad], axis=1).astype(BF16)
    wt = jnp.concatenate([q, qi, v], axis=1).T.astype(BF16)
    return (w, wt, wi.T.astype(BF16), w_out[:D_CONV].astype(BF16), w_out[D_CONV:].astype(BF16))


def _prep_odd(w_in, b_i, b_f):
    q, k, v, o, wg = jnp.split(w_in, [_OD_K, _OD_V, _OD_O, _OD_END], axis=1)
    w = jnp.concatenate([k, o], axis=1).astype(BF16)
    wt = jnp.concatenate([q, v], axis=1).T.astype(BF16)
    wg = wg.astype(BF16)
    bg = jnp.concatenate([b_i, b_f])
    return w, wt, wg, wg.T, bg[None, :], bg[:, None]


def _tile(n, pref):
    return pref if n % pref == 0 else n


def _pad_axis(a, axis, n):
    if a.shape[axis] == n:
        return a
    widths = [(0, 0)] * a.ndim
    widths[axis] = (0, n - a.shape[axis])
    return jnp.pad(a, widths)


def _tail(x2, qx, mk, mv, wo, g3, g4, g5, w1, w2, *, b, t):
    t_pad = max(t, 2 * SUBLANES)
    tx = _tile(t_pad, 512)
    q3 = _pad_axis(qx.reshape(b, t, -1), 1, t_pad)
    x3 = _pad_axis(x2.reshape(b, t, D_MODEL), 1, t_pad)
    x3 = _xattn(q3, mk, mv, x3, wo, g3, tm=tx)[:, :t].reshape(b * t, D_MODEL)
    return _mlp(x3, g4, g5, w1, w2, tm=_tile(b * t, 1024), tf=1024)


def _even_prompt(x2, g, we, cw, wxq, *, b, t, tm):
    w, wt, wwi, wo_a, wo_b = we
    tq = _tile(t, 256)
    tabs = _rope_tables(jnp.arange(t))
    tabs_t = tuple(tab.T for tab in tabs)
    init = jnp.zeros((b, CONV_W - 1, D_CONV), F32)
    ya, k, v, ki, kb, kib, qt, qit, vt, wit, conv = _even_proj_prompt(
        x2, g[0:1], w, wt, wwi, cw, tabs, tabs_t, init, tm=tm, seq_len=t, kc=tq)
    yb = _dsa_prompt(qit, wit, kib, qt, kb, vt, b=b, t=t, tq=tq)
    x1, qx = _mix_out([ya, yb], [wo_a, wo_b], x2, g[1:2], g[2:3], wxq, tm=tm)
    state = (k.reshape(b, t, N_KV_B, HD_B), v.reshape(b, t, N_KV_B, HD_B),
             ki.reshape(b, t, IDX_DIM), conv)
    return x1, qx, state


def _odd_mixer(x2, g, wod, hg, w_out, wxq, c0, n0, m0, *, b, t, tm, lc):
    w, wt, wg, wgt, bg, bgt = wod
    qt, k, vt, so, gate, gatet = _odd_proj(x2, g[0:1], w, wt, wg, wgt, bg, bgt, tm=tm)
    nh = N_HEADS_C
    tp = -(-t // lc) * lc
    if tp > t:
        rows = lambda a: _pad_axis(a.reshape(b, t, -1), 1, tp).reshape(b * tp, -1)
        cols = lambda a: _pad_axis(a.reshape(-1, b, t), 2, tp).reshape(-1, b * tp)
        neutral = jnp.concatenate([jnp.full((nh,), NEG_BIG, F32), jnp.zeros((nh,), F32)])
        gate = jnp.concatenate(
            [gate.reshape(b, t, 2 * nh),
             jnp.broadcast_to(neutral[None, None, :], (b, tp - t, 2 * nh))], axis=1)
        gatet = jnp.concatenate(
            [gatet.reshape(2 * nh, b, t),
             jnp.broadcast_to(neutral[:, None, None], (2 * nh, b, tp - t))], axis=2)
        qt, k, vt, so = cols(qt), rows(k), cols(vt), rows(so)
        gate, gatet = gate.reshape(b * tp, 2 * nh), gatet.reshape(2 * nh, b * tp)
    c0p = c0.reshape(b, nh // 2, 2, DV_C, DQK_C).transpose(0, 1, 3, 2, 4)
    c0p = c0p.reshape(b, nh // 2, DV_C, 2 * DQK_C)
    h, cp, n, m = _mlstm(qt, k, vt, so, gate, gatet, hg, c0p, n0.reshape(b, nh // 2, 2 * DQK_C),
                         m0[:, :, None], b=b, t=tp, lc=lc)
    h2 = h.reshape(b, tp, nh * DV_C)[:, :t].reshape(b * t, nh * DV_C)
    x1, qx = _mix_out([h2], [w_out], x2, g[1:2], g[2:3], wxq, tm=tm)
    c_out = cp.reshape(b, nh // 2, DV_C, 2, DQK_C).transpose(0, 1, 3, 2, 4)
    c_out = c_out.reshape(b, nh, DV_C, DQK_C)
    return x1, qx, (c_out, n.reshape(b, nh, DQK_C), m[:, :, 0])


def _even_sample(x2, g, we, cw, wxq, cache_i, cache_k, cache_v, state_conv, page_table, *, b, t):
    w, _, wwi, wo_a, wo_b = we
    n_pages = page_table.shape[1]
    pos = n_pages * PAGE_SIZE + jnp.arange(t)
    tabs = tuple(jnp.repeat(tab, b, axis=0) for tab in _rope_tables(pos))
    to_tm = lambda a: a.reshape(b, t, -1).transpose(1, 0, 2).reshape(t * b, -1)
    to_bm = lambda a: a.reshape(t, b, -1).transpose(1, 0, 2)
    init = state_conv.transpose(1, 0, 2).reshape(2 * b, D_CONV)
    ya, q, k, v, qi, ki, wit, conv = _even_proj(
        to_tm(x2), g[0:1], w, wwi, cw, tabs, init, seq_len=t, time_major_b=b)
    ng = N_HEADS_B // N_KV_B
    qi_b = to_bm(qi).reshape(b, t, N_IDX_HEADS, IDX_DIM).transpose(0, 2, 1, 3)
    qi_b = _pad_axis(qi_b, 2, _TP).reshape(b, N_IDX_HEADS * _TP, IDX_DIM)
    w_b = _pad_axis(wit.reshape(N_IDX_HEADS, t, b).transpose(2, 0, 1), 2, _TP)
    w_b = w_b.reshape(b, N_IDX_HEADS * _TP, 1)
    new_t = lambda a: _pad_axis(to_bm(a), 1, _NB).transpose(0, 2, 1).astype(BF16)
    kin, kn, vn = new_t(ki), new_t(k), new_t(v)
    q5 = _pad_axis(to_bm(q).reshape(b, t, N_KV_B, ng, HD_B).transpose(0, 2, 3, 1, 4), 3, _TP)
    zq = jnp.zeros_like(q5[:, 0])
    qp = jnp.stack([jnp.concatenate([q5[:, 0], zq], axis=-1),
                    jnp.concatenate([zq, q5[:, 1]], axis=-1)], axis=1)
    qp = qp.reshape(b, N_KV_B * ng * _TP, N_KV_B * HD_B)
    n_phys = cache_k.shape[0]
    pool_t = lambda a: a.reshape(n_phys, PAGE_SIZE, -1).transpose(0, 2, 1)
    o = _dsa_sample(page_table, qi_b, w_b, kin, qp, kn, vn,
                    pool_t(cache_i), pool_t(cache_k), pool_t(cache_v), t_new=t)
    o = o.reshape(b, N_KV_B, ng, _TP, N_KV_B, HD_B)[:, :, :, :t]
    yb = jnp.stack([o[:, 0, :, :, 0], o[:, 1, :, :, 1]], axis=1)
    yb = yb.transpose(3, 0, 1, 2, 4).reshape(t * b, N_HEADS_B * HD_B).astype(BF16)
    x1, qx = _mix_out([ya, yb], [wo_a, wo_b], to_tm(x2), g[1:2], g[2:3], wxq, tm=t * b)
    back = lambda a: to_bm(a).reshape(b * t, -1)
    state = (to_bm(k).reshape(b, t, N_KV_B, HD_B), to_bm(v).reshape(b, t, N_KV_B, HD_B),
             to_bm(ki), conv.reshape(2, b, D_CONV).transpose(1, 0, 2))
    return back(x1), back(qx), state


def kernel(x_prompt, x_sample, cache_k, cache_v, cache_idx_k, cache_mem_k, cache_mem_v, state_conv, state_C, state_n, state_m, page_table, mem_prompt, norm_g, w_in_even, conv_w, w_out_even, w_in_odd, b_i, b_f, hnorm_g, w_out_odd, mem_norm_g, w_xq, w_xk, w_xv, w_xo, w_ff1, w_ff2):
    bp, tp, _ = x_prompt.shape
    bs, ts, _ = x_sample.shape
    depth = norm_g.shape[0]
    tm_p = _tile(tp, 512)
    lc = _tile(tp, 256)
    nmem = mem_prompt.shape[1]
    mk_all, mv_all = _mem_kv(mem_prompt.reshape(bp * nmem, D_MODEL), mem_norm_g[:, None, :],
                             w_xk.astype(BF16), w_xv.astype(BF16), tm=_tile(bp * nmem, 512))
    xp = x_prompt.reshape(bp * tp, D_MODEL)
    xs = x_sample.reshape(bs * ts, D_MODEL)
    ev_p, ev_s, od_p, od_s = [], [], [], []
    for l in range(depth):
        g = norm_g[l]
        wxq = w_xq[l].astype(BF16)
        if l % 2 == 0:
            e = l // 2
            we = _prep_even(w_in_even[e], w_out_even[e])
            xp, qxp, st = _even_prompt(xp, g, we, conv_w[e], wxq, b=bp, t=tp, tm=tm_p)
            ev_p.append(st)
            xs, qxs, st = _even_sample(xs, g, we, conv_w[e], wxq, cache_idx_k[e], cache_k[e],
                                       cache_v[e], state_conv[e], page_table, b=bs, t=ts)
            ev_s.append(st)
        else:
            o = l // 2
            wod = _prep_odd(w_in_odd[o], b_i[o], b_f[o])
            hg = hnorm_g[o][None, :]
            wout = w_out_odd[o].astype(BF16)
            zc = jnp.zeros((bp, N_HEADS_C, DV_C, DQK_C), F32)
            zn = jnp.zeros((bp, N_HEADS_C, DQK_C), F32)
            zm = jnp.zeros((bp, N_HEADS_C), F32)
            xp, qxp, st = _odd_mixer(xp, g, wod, hg, wout, wxq, zc, zn, zm, b=bp, t=tp, tm=tm_p, lc=lc)
            od_p.append(st)
            xs, qxs, st = _odd_mixer(xs, g, wod, hg, wout, wxq, state_C[o], state_n[o], state_m[o],
                                     b=bs, t=ts, tm=bs * ts, lc=LANES)
            od_s.append(st)
        wxo = w_xo[l].astype(BF16)
        w1 = w_ff1[l].astype(BF16)
        w2 = w_ff2[l].astype(BF16)
        nx = N_HEADS_X * HD_X
        xp = _tail(xp, qxp, mk_all[l].reshape(bp, nmem, nx), mv_all[l].reshape(bp, nmem, nx),
                   wxo, g[3:4], g[4:5], g[5:6], w1, w2, b=bp, t=tp)
        xs = _tail(xs, qxs, cache_mem_k[l].reshape(bs, nmem, nx), cache_mem_v[l].reshape(bs, nmem, nx),
                   wxo, g[3:4], g[4:5], g[5:6], w1, w2, b=bs, t=ts)
    stack = lambda sts, j: jnp.stack([s[j] for s in sts])
    mem_shape = (depth, bp, nmem, N_HEADS_X, HD_X)
    return (xp.reshape(bp, tp, D_MODEL), xs.reshape(bs, ts, D_MODEL),
            stack(ev_p, 0), stack(ev_p, 1), stack(ev_p, 2), stack(ev_p, 3),
            stack(od_p, 0), stack(od_p, 1), stack(od_p, 2),
            mk_all.reshape(mem_shape), mv_all.reshape(mem_shape),
            stack(ev_s, 0), stack(ev_s, 1), stack(ev_s, 2), stack(ev_s, 3),
            stack(od_s, 0), stack(od_s, 1), stack(od_s, 2))
```

```python
import functools
import math

import jax
import jax.numpy as jnp
from jax import lax
from jax.experimental import pallas as pl
from jax.experimental.pallas import tpu as pltpu

F32 = jnp.float32
BF16 = jnp.bfloat16
I32 = jnp.int32

D_MODEL = 1024
D_CONV = 512
CONV_W = 3
N_HEADS_B = 8
N_KV_B = 2
HD_B = 64
N_IDX_HEADS = 8
IDX_DIM = 64
TOPK_MAX = 256
N_HEADS_C = 8
DQK_C = 64
DV_C = 128
N_MEM = 256
N_HEADS_X = 4
HD_X = 128
D_FF = 4096
PAGE_SIZE = 128
ROPE_THETA = 500000.0
NORM_EPS = 1e-6

LANES = 128
SUBLANES = 8
VMEM_LIMIT = 48 * 1024 * 1024

NEG_BIG = -1e30
INT_MIN = -(2 ** 31)
KEY_NEG_INF = -(2 ** 31) + 0x007FFFFF

_EV_U, _EV_GB, _EV_GC, _EV_Q, _EV_K, _EV_V, _EV_QI, _EV_KI, _EV_END = (
    0, 512, 1024, 1536, 2048, 2176, 2304, 2816, 2944)
_OD_Q, _OD_K, _OD_V, _OD_O, _OD_END = 0, 512, 1024, 2048, 3072


def _cparams(sem, vmem=VMEM_LIMIT):
    return pltpu.CompilerParams(dimension_semantics=sem, vmem_limit_bytes=vmem)


def _rms(x, g):
    return x * lax.rsqrt(jnp.mean(x * x, axis=-1, keepdims=True) + NORM_EPS) * g


def _dot(a, b):
    return jnp.dot(a, b, preferred_element_type=F32)


def _dot_nt(a, b):
    return lax.dot_general(a, b, (((1,), (1,)), ((), ())), preferred_element_type=F32)


def _rope128(x, c, sa, sb):
    return x * c + pltpu.roll(x, LANES - 8, 1) * sa + pltpu.roll(x, 8, 1) * sb


def _rope_tables(pos):
    r = HD_B // 4
    half = r // 2
    freqs = ROPE_THETA ** (-jnp.arange(half, dtype=F32) * 2.0 / r)
    ang = pos.astype(F32)[:, None] * freqs[None, :]
    cos, sin = jnp.cos(ang), jnp.sin(ang)
    t = pos.shape[0]
    ones = jnp.ones((t, HD_B - r), F32)
    zeros = jnp.zeros((t, HD_B - r), F32)
    zh = jnp.zeros((t, half), F32)
    c = jnp.concatenate([cos, cos, ones], axis=1)
    sa = jnp.concatenate([-sin, zh, zeros], axis=1)
    sb = jnp.concatenate([zh, sin, zeros], axis=1)
    tile2 = lambda a: jnp.concatenate([a, a], axis=1)
    return tile2(c), tile2(sa), tile2(sb)


_Q_SCALE = HD_B ** -0.5 * math.log2(math.e)
_ONES_ROWS = 16
_BOUND_SLACK = 1.01
_MIN_MASS = 2.0 ** -100


def _rope128_t(x, c, sa, sb):
    return x * c + pltpu.roll(x, LANES - 8, 0) * sa + pltpu.roll(x, 8, 0) * sb


def _even_proj_body(x_ref, g_ref, w_ref, wwi_ref, cw_ref, cos_ref, sa_ref, sb_ref, init_ref,
                    ya_ref, q_ref, k_ref, v_ref, qi_ref, ki_ref, wit_ref, conv_ref, *, time_major_b):
    tm = x_ref.shape[0]
    xn = _rms(x_ref[...], g_ref[...]).astype(BF16)
    c, sa, sb = cos_ref[...], sa_ref[...], sb_ref[...]

    def seg(a, b):
        return _dot(xn, w_ref[:, a:b])

    z = seg(_EV_GC, _EV_Q) * seg(_EV_U, _EV_GB)
    cw = cw_ref[...]
    nb = time_major_b
    init = init_ref[...]
    z1 = jnp.concatenate([init[nb:2 * nb], z[:tm - nb]], axis=0)
    z2 = jnp.concatenate([init, z[:tm - 2 * nb]], axis=0)
    conv_ref[...] = z[tm - 2 * nb:, :]
    conv = cw[0:1] * z2 + cw[1:2] * z1 + cw[2:3] * z
    ya_ref[...] = (seg(_EV_GB, _EV_GC) * conv).astype(BF16)

    for j in range(4):
        a = _EV_Q + j * LANES
        q_ref[:, j * LANES:(j + 1) * LANES] = (
            _rope128(seg(a, a + LANES), c, sa, sb) * _Q_SCALE).astype(BF16)
        a = _EV_QI + j * LANES
        qi_ref[:, j * LANES:(j + 1) * LANES] = (
            _rope128(seg(a, a + LANES), c, sa, sb) * IDX_DIM ** -0.5).astype(BF16)
    k_ref[...] = _rope128(seg(_EV_K, _EV_V), c, sa, sb)
    v_ref[...] = seg(_EV_V, _EV_QI)
    ki_ref[...] = _rope128(seg(_EV_KI, _EV_END), c, sa, sb)[:, :IDX_DIM]
    wit_ref[...] = _dot_nt(wwi_ref[...], xn) * N_IDX_HEADS ** -0.5


def _even_proj_prompt_body(x_ref, g_ref, w_ref, wt_ref, cw_ref, cos_ref, sa_ref, sb_ref,
                           cost_ref, sat_ref, sbt_ref, init_ref,
                           ya_ref, k_ref, v_ref, ki_ref, kb_ref, kib_ref, qt_ref, qit_ref, vt_ref,
                           wit_ref, conv_ref, carry_ref, *, tiles_per_seq):
    tm = x_ref.shape[0]
    xn = _rms(x_ref[...], g_ref[...]).astype(BF16)
    c, sa, sb = cos_ref[...], sa_ref[...], sb_ref[...]
    ct, sat, sbt = cost_ref[...], sat_ref[...], sbt_ref[...]

    def seg(a, b):
        return _dot(xn, w_ref[:, a:b])

    z = seg(_EV_GC, _EV_Q) * seg(_EV_U, _EV_GB)
    cw = cw_ref[...]

    @pl.when(pl.program_id(0) % tiles_per_seq == 0)
    def _():
        carry_ref[0:2, :] = init_ref[0]

    c0 = carry_ref[0:1, :]
    c1 = carry_ref[1:2, :]
    row = lax.broadcasted_iota(I32, (tm, 1), 0)
    z1 = jnp.where(row == 0, c1, pltpu.roll(z, 1, 0))
    z2 = jnp.where(row == 0, c0, jnp.where(row == 1, c1, pltpu.roll(z, 2, 0)))
    carry_ref[0:2, :] = z[tm - 2:tm, :]
    conv_ref[0] = z[tm - 2:tm, :]
    conv = cw[0:1] * z2 + cw[1:2] * z1 + cw[2:3] * z
    ya_ref[...] = (seg(_EV_GB, _EV_GC) * conv).astype(BF16)

    k = _rope128(seg(_EV_K, _EV_V), c, sa, sb)
    k_ref[...] = k
    kb_ref[...] = k.astype(BF16)
    v_ref[...] = seg(_EV_V, _EV_QI)
    ki = _rope128(seg(_EV_KI, _EV_END), c, sa, sb)[:, :IDX_DIM]
    ki_ref[...] = ki
    kib_ref[...] = ki.astype(BF16)

    nq = N_HEADS_B * HD_B
    nqi = N_IDX_HEADS * IDX_DIM
    allt = _dot_nt(wt_ref[...], xn)
    for j in range(nq // LANES):
        xt = allt[j * LANES:(j + 1) * LANES]
        qt_ref[j * LANES:(j + 1) * LANES, :] = (_rope128_t(xt, ct, sat, sbt) * _Q_SCALE).astype(BF16)
    for j in range(nqi // LANES):
        xt = allt[nq + j * LANES:nq + (j + 1) * LANES]
        qit_ref[j * LANES:(j + 1) * LANES, :] = (
            _rope128_t(xt, ct, sat, sbt) * IDX_DIM ** -0.5).astype(BF16)
    nkv = N_KV_B * HD_B
    vt = allt[nq + nqi:nq + nqi + nkv].astype(BF16)
    kc = vt_ref.shape[3]
    ones = jnp.ones((_ONES_ROWS, kc), BF16)
    for j in range(tm // kc):
        for g in range(N_KV_B):
            vt_ref[j, g] = jnp.concatenate(
                [vt[g * HD_B:(g + 1) * HD_B, j * kc:(j + 1) * kc], ones], axis=0)
    wit_ref[...] = allt[nq + nqi + nkv:nq + nqi + nkv + N_IDX_HEADS] * N_IDX_HEADS ** -0.5


def _even_proj_prompt(x, g, w, wt, cw, tabs, tabs_t, init, *, tm, seq_len, kc):
    m = x.shape[0]
    tiles_per_seq = seq_len // tm
    row = lambda i: (i, 0)
    col = lambda i: (0, i)
    const = lambda i: (0, 0)
    seq = lambda i: (i // tiles_per_seq, 0, 0)
    nq = N_HEADS_B * HD_B
    nqi = N_IDX_HEADS * IDX_DIM
    nkv = N_KV_B * HD_B
    tab_spec = pl.BlockSpec((tm, LANES), lambda i: (i % tiles_per_seq, 0))
    tabt_spec = pl.BlockSpec((LANES, tm), lambda i: (0, i % tiles_per_seq))
    vrows = HD_B + _ONES_ROWS
    in_specs = [
        pl.BlockSpec((tm, D_MODEL), row), pl.BlockSpec((1, D_MODEL), const),
        pl.BlockSpec((D_MODEL, _EV_END), const), pl.BlockSpec(wt.shape, const),
        pl.BlockSpec((CONV_W, D_CONV), const),
        tab_spec, tab_spec, tab_spec, tabt_spec, tabt_spec, tabt_spec,
        pl.BlockSpec((1, 2, D_CONV), seq),
    ]
    out_specs = (
        pl.BlockSpec((tm, D_CONV), row),
        pl.BlockSpec((tm, nkv), row), pl.BlockSpec((tm, nkv), row), pl.BlockSpec((tm, IDX_DIM), row),
        pl.BlockSpec((tm, nkv), row), pl.BlockSpec((tm, IDX_DIM), row),
        pl.BlockSpec((nq, tm), col), pl.BlockSpec((nqi, tm), col),
        pl.BlockSpec((tm // kc, N_KV_B, vrows, kc), lambda i: (i, 0, 0, 0)),
        pl.BlockSpec((N_IDX_HEADS, tm), col),
        pl.BlockSpec((1, 2, D_CONV), seq),
    )
    out_shape = (
        jax.ShapeDtypeStruct((m, D_CONV), BF16),
        jax.ShapeDtypeStruct((m, nkv), F32), jax.ShapeDtypeStruct((m, nkv), F32),
        jax.ShapeDtypeStruct((m, IDX_DIM), F32),
        jax.ShapeDtypeStruct((m, nkv), BF16), jax.ShapeDtypeStruct((m, IDX_DIM), BF16),
        jax.ShapeDtypeStruct((nq, m), BF16), jax.ShapeDtypeStruct((nqi, m), BF16),
        jax.ShapeDtypeStruct((m // kc, N_KV_B, vrows, kc), BF16),
        jax.ShapeDtypeStruct((N_IDX_HEADS, m), F32),
        jax.ShapeDtypeStruct((m // seq_len, 2, D_CONV), F32),
    )
    return pl.pallas_call(
        functools.partial(_even_proj_prompt_body, tiles_per_seq=tiles_per_seq),
        grid=(m // tm,), in_specs=in_specs, out_specs=out_specs, out_shape=out_shape,
        scratch_shapes=[pltpu.VMEM((SUBLANES, D_CONV), F32)],
        compiler_params=_cparams(("arbitrary",)), name="even_proj_prompt",
    )(x, g, w, wt, cw, *tabs, *tabs_t, init)


def _even_proj(x, g, w, wwi, cw, tabs, init, *, seq_len, time_major_b):
    m = x.shape[0]
    tm = m
    nt = 1
    assert seq_len >= 2
    row = lambda i: (i, 0)
    const = lambda i: (0, 0)
    tab_spec = pl.BlockSpec((tm, LANES), const)
    init_spec = pl.BlockSpec((2 * time_major_b, D_CONV), const)
    conv_shape = jax.ShapeDtypeStruct((2 * time_major_b, D_CONV), F32)
    conv_spec = pl.BlockSpec((2 * time_major_b, D_CONV), const)
    scratch = []
    out_shape = (
        jax.ShapeDtypeStruct((m, D_CONV), BF16),
        jax.ShapeDtypeStruct((m, N_HEADS_B * HD_B), BF16),
        jax.ShapeDtypeStruct((m, N_KV_B * HD_B), F32),
        jax.ShapeDtypeStruct((m, N_KV_B * HD_B), F32),
        jax.ShapeDtypeStruct((m, N_IDX_HEADS * IDX_DIM), BF16),
        jax.ShapeDtypeStruct((m, IDX_DIM), F32),
        jax.ShapeDtypeStruct((N_IDX_HEADS, m), F32),
        conv_shape,
    )
    out_specs = (
        pl.BlockSpec((tm, D_CONV), row),
        pl.BlockSpec((tm, N_HEADS_B * HD_B), row),
        pl.BlockSpec((tm, N_KV_B * HD_B), row),
        pl.BlockSpec((tm, N_KV_B * HD_B), row),
        pl.BlockSpec((tm, N_IDX_HEADS * IDX_DIM), row),
        pl.BlockSpec((tm, IDX_DIM), row),
        pl.BlockSpec((N_IDX_HEADS, tm), lambda i: (0, i)),
        conv_spec,
    )
    in_specs = [
        pl.BlockSpec((tm, D_MODEL), row),
        pl.BlockSpec((1, D_MODEL), const),
        pl.BlockSpec((D_MODEL, _EV_END), const),
        pl.BlockSpec((N_IDX_HEADS, D_MODEL), const),
        pl.BlockSpec((CONV_W, D_CONV), const),
        tab_spec, tab_spec, tab_spec,
        init_spec,
    ]
    return pl.pallas_call(
        functools.partial(_even_proj_body, time_major_b=time_major_b),
        grid=(nt,), in_specs=in_specs, out_specs=out_specs, out_shape=out_shape,
        scratch_shapes=scratch, compiler_params=_cparams(("arbitrary",)),
        name="even_proj",
    )(x, g, w, wwi, cw, *tabs, init)


def _mix_out_body(*refs, n_in):
    ins = refs[:n_in]
    ws = refs[n_in:2 * n_in]
    x_ref, g1_ref, g2_ref, wq_ref, x1_ref, qx_ref = refs[2 * n_in:]
    y = _dot(ins[0][...], ws[0][...])
    for a, w in zip(ins[1:], ws[1:]):
        y = y + _dot(a[...], w[...])
    x1 = x_ref[...] + _rms(y, g1_ref[...])
    x1_ref[...] = x1
    xn = _rms(x1, g2_ref[...]).astype(BF16)
    qx_ref[...] = (_dot(xn, wq_ref[...]) * HD_X ** -0.5).astype(BF16)


def _mix_out(ins, ws, x, g1, g2, wq, *, tm):
    m = x.shape[0]
    row = lambda i: (i, 0)
    const = lambda i: (0, 0)
    n_in = len(ins)
    in_specs = ([pl.BlockSpec((tm, a.shape[1]), row) for a in ins]
                + [pl.BlockSpec(w.shape, const) for w in ws]
                + [pl.BlockSpec((tm, D_MODEL), row), pl.BlockSpec((1, D_MODEL), const),
                   pl.BlockSpec((1, D_MODEL), const), pl.BlockSpec(wq.shape, const)])
    nq = wq.shape[1]
    return pl.pallas_call(
        functools.partial(_mix_out_body, n_in=n_in),
        grid=(m // tm,), in_specs=in_specs,
        out_specs=(pl.BlockSpec((tm, D_MODEL), row), pl.BlockSpec((tm, nq), row)),
        out_shape=(jax.ShapeDtypeStruct((m, D_MODEL), F32), jax.ShapeDtypeStruct((m, nq), BF16)),
        compiler_params=_cparams(("parallel",)), name="mix_out",
    )(*ins, *ws, x, g1, g2, wq)


def _mem_kv_body(mem_ref, g_ref, wk_ref, wv_ref, mk_ref, mv_ref):
    mn = _rms(mem_ref[...], g_ref[0]).astype(BF16)
    mk_ref[0] = _dot(mn, wk_ref[0])
    mv_ref[0] = _dot(mn, wv_ref[0])


def _mem_kv(mem, g, wk, wv, *, tm):
    m = mem.shape[0]
    depth = g.shape[0]
    n = wk.shape[2]
    wspec = pl.BlockSpec((1, D_MODEL, n), lambda l, i: (l, 0, 0))
    ospec = pl.BlockSpec((1, tm, n), lambda l, i: (l, i, 0))
    oshape = jax.ShapeDtypeStruct((depth, m, n), F32)
    return pl.pallas_call(
        _mem_kv_body, grid=(depth, m // tm),
        in_specs=[pl.BlockSpec((tm, D_MODEL), lambda l, i: (i, 0)),
                  pl.BlockSpec((1, 1, D_MODEL), lambda l, i: (l, 0, 0)), wspec, wspec],
        out_specs=(ospec, ospec), out_shape=(oshape, oshape),
        compiler_params=_cparams(("parallel", "parallel")), name="mem_kv",
    )(mem, g, wk, wv)


def _xattn_body(q_ref, mk_ref, mv_ref, x_ref, wo_ref, g_ref, o_ref):
    bb, tm = q_ref.shape[0], q_ref.shape[1]
    rows = []
    for j in range(bb):
        q = q_ref[j]
        mk = mk_ref[j].astype(BF16)
        mv = mv_ref[j].astype(BF16)
        outs = []
        for h in range(N_HEADS_X):
            sl = slice(h * HD_X, (h + 1) * HD_X)
            s = _dot_nt(q[:, sl], mk[:, sl])
            s = s - jnp.max(s, axis=-1, keepdims=True)
            p = jnp.exp(s)
            p = p / jnp.sum(p, axis=-1, keepdims=True)
            outs.append(_dot(p.astype(BF16), mv[:, sl]))
        rows.append(jnp.concatenate(outs, axis=-1).astype(BF16))
    o = rows[0] if bb == 1 else jnp.concatenate(rows, axis=0)
    x = x_ref[...].reshape(bb * tm, D_MODEL)
    o_ref[...] = (x + _rms(_dot(o, wo_ref[...]), g_ref[...])).reshape(bb, tm, D_MODEL)


def _xattn(q, mk, mv, x, wo, g, *, tm, bb):
    b, t, _ = x.shape
    nq = q.shape[2]
    tile = lambda i, j: (i, j, 0)
    per_b = lambda i, j: (i, 0, 0)
    const = lambda i, j: (0, 0)
    return pl.pallas_call(
        _xattn_body, grid=(b // bb, t // tm),
        in_specs=[pl.BlockSpec((bb, tm, nq), tile),
                  pl.BlockSpec((bb, N_MEM, nq), per_b), pl.BlockSpec((bb, N_MEM, nq), per_b),
                  pl.BlockSpec((bb, tm, D_MODEL), tile),
                  pl.BlockSpec(wo.shape, const), pl.BlockSpec((1, D_MODEL), const)],
        out_specs=pl.BlockSpec((bb, tm, D_MODEL), tile),
        out_shape=jax.ShapeDtypeStruct((b, t, D_MODEL), F32),
        compiler_params=_cparams(("parallel", "parallel")), name="xattn",
    )(q, mk, mv, x, wo, g)


def _mlp_body(x_ref, g4_ref, g5_ref, w1_ref, w2_ref, o_ref):
    x = x_ref[...]
    xn = _rms(x, g4_ref[...]).astype(BF16)
    h = jnp.maximum(_dot(xn, w1_ref[...]), 0.0)
    y = _dot((h * h).astype(BF16), w2_ref[...])
    o_ref[...] = x + _rms(y, g5_ref[...])


def _mlp(x, g4, g5, w1, w2, *, tm):
    m = x.shape[0]
    row = lambda i: (i, 0)
    const = lambda i: (0, 0)
    resident = lambda a: pl.BlockSpec(a.shape, const, pipeline_mode=pl.Buffered(1))
    return pl.pallas_call(
        _mlp_body, grid=(m // tm,),
        in_specs=[pl.BlockSpec((tm, D_MODEL), row), pl.BlockSpec((1, D_MODEL), const),
                  pl.BlockSpec((1, D_MODEL), const), resident(w1), resident(w2)],
        out_specs=pl.BlockSpec((tm, D_MODEL), row),
        out_shape=jax.ShapeDtypeStruct((m, D_MODEL), F32),
        compiler_params=_cparams(("parallel",)), name="mlp",
    )(x, g4, g5, w1, w2)


def _for_pairs(n, body):
    def pair(j, carry):
        body(2 * j, carry)
        body(2 * j + 1, carry)
        return carry

    lax.fori_loop(0, n // 2, pair, 0)

    @pl.when(n % 2 == 1)
    def _():
        body(n - 1, 0)


def _sortable_key(score):
    bits = pltpu.bitcast(score, I32)
    return bits ^ ((bits >> 31) & 0x7FFFFFFF)


def _dsa_prompt_body(qit_ref, wit_ref, ki_ref, qt_ref, k_ref, vt_ref, o_ref,
                     key_ref, qpad_ref, m_ref, acc_ref, knorm_ref, *, ktop, idx_bits):
    tq = qit_ref.shape[1]
    kc = tq
    i = pl.program_id(1)
    nck = i + 1
    kiota = lax.broadcasted_iota(I32, (kc, tq), 0)
    qidx = i * tq + lax.broadcasted_iota(I32, (kc, tq), 1)
    ng = N_HEADS_B // N_KV_B

    def chunk_off(c):
        return pl.multiple_of(c * kc, kc)

    def score_chunk(c, carry):
        off = chunk_off(c)
        kic = ki_ref[pl.ds(off, kc), :]
        sc = jnp.zeros((kc, tq), F32)
        for h in range(N_IDX_HEADS):
            s = _dot(kic, qit_ref[h * IDX_DIM:(h + 1) * IDX_DIM, :])
            sc = sc + wit_ref[h:h + 1, :] * jnp.maximum(s, 0.0)
        sc = jnp.where(kiota + off <= qidx, sc, -jnp.inf)
        key_ref[pl.ds(off, kc), :] = _sortable_key(sc)
        return carry

    _for_pairs(nck, score_chunk)

    @pl.when(nck % 2 == 1)
    def _():
        key_ref[pl.ds(chunk_off(nck), kc), :] = jnp.full((kc, tq), INT_MIN, I32)

    def count(pred):
        def one(c, acc):
            off = chunk_off(c)
            hit = pred(key_ref[pl.ds(off, kc), :], kiota + off).astype(I32)
            return acc + jnp.sum(hit.reshape(kc // SUBLANES, SUBLANES, tq), axis=0)

        def body(j, acc):
            return one(2 * j + 1, one(2 * j, acc))

        acc = lax.fori_loop(0, (nck + 1) // 2, body, jnp.zeros((SUBLANES, tq), I32))
        return jnp.sum(acc, axis=0, keepdims=True)

    def bit_step(j, carry):
        lo, cnt_lo = carry
        cand = lo + jnp.left_shift(jnp.int32(1), 31 - j)
        cnt = count(lambda blk, _: blk >= cand)
        keep = cnt >= ktop
        return jnp.where(keep, cand, lo), jnp.where(keep, cnt, cnt_lo)

    thr, cnt_ge = lax.fori_loop(
        0, 32, bit_step, (jnp.full((1, tq), INT_MIN, I32), jnp.full((1, tq), ktop, I32)))

    tie = jnp.logical_and(thr > KEY_NEG_INF, cnt_ge > ktop)

    @pl.when(jnp.max(tie.astype(I32)) > 0)
    def _():
        need = ktop - count(lambda blk, _: blk > thr)

        def idx_step(j, res):
            cand = res + jnp.left_shift(jnp.int32(1), idx_bits - 1 - j)
            cnt = count(lambda blk, kidx: jnp.logical_and(blk == thr, kidx < cand))
            return jnp.where(cnt < need, cand, res)

        jcut = lax.fori_loop(0, idx_bits, idx_step, jnp.zeros((1, tq), I32))

        def drop_chunk(c, carry):
            off = chunk_off(c)
            blk = key_ref[pl.ds(off, kc), :]
            drop = jnp.logical_and(blk == thr, kiota + off > jcut)
            key_ref[pl.ds(off, kc), :] = jnp.where(drop, blk - 1, blk)
            return carry

        lax.fori_loop(0, nck, drop_chunk, 0)

    thr_eff = jnp.maximum(thr, KEY_NEG_INF + 1)

    @pl.when(i == 0)
    def _():
        def knorm_chunk(c, mx):
            kk = k_ref[pl.ds(chunk_off(c), kc), :].astype(F32)
            sq = kk * kk
            per_g = [jnp.max(jnp.sum(sq[:, g * HD_B:(g + 1) * HD_B], axis=1, keepdims=True),
                             axis=0, keepdims=True) for g in range(N_KV_B)]
            return jnp.maximum(mx, jnp.concatenate(per_g, axis=0))

        mx = lax.fori_loop(0, k_ref.shape[0] // kc, knorm_chunk, jnp.zeros((N_KV_B, 1), F32))
        knorm_ref[...] = jnp.broadcast_to(mx, knorm_ref.shape)

    zero_half = jnp.zeros((HD_B, tq), BF16)
    for g in range(N_KV_B):
        cols, bounds = [], []
        for hh in range(ng):
            h = g * ng + hh
            qh = qt_ref[h * HD_B:(h + 1) * HD_B, :]
            halves = [zero_half] * N_KV_B
            halves[g] = qh
            cols.append(jnp.concatenate(halves, axis=0))
            qf = qh.astype(F32)
            qn2 = jnp.sum(qf * qf, axis=0, keepdims=True)
            bounds.append(jnp.sqrt(qn2 * knorm_ref[g:g + 1, 0:1]) * _BOUND_SLACK)
        qpad_ref[g, 0:N_KV_B * HD_B, :] = jnp.concatenate(cols, axis=1)
        neg_bound = jnp.broadcast_to(-jnp.concatenate(bounds, axis=1), (2 * SUBLANES, ng * tq))
        qpad_ref[g, N_KV_B * HD_B:N_KV_B * HD_B + 2 * SUBLANES, :] = jnp.where(
            lax.broadcasted_iota(I32, neg_bound.shape, 0) == 0, neg_bound, 0.0).astype(BF16)
    acc_ref[...] = jnp.zeros(acc_ref.shape, F32)
    ones_cols = jnp.ones((kc, 2 * SUBLANES), BF16)

    def chunk_inputs(c):
        off = chunk_off(c)
        bias = jnp.where(key_ref[pl.ds(off, kc), :] >= thr_eff, 0.0, NEG_BIG)
        return jnp.concatenate([bias] * ng, axis=1), k_ref[pl.ds(off, kc), :]

    def attn_chunk(c, carry):
        bias, kch = chunk_inputs(c)
        kaug = jnp.concatenate([kch, ones_cols], axis=1)
        for g in range(N_KV_B):
            p = jnp.exp2(_dot(kaug, qpad_ref[g]) + bias).astype(BF16)
            acc_ref[g] += _dot(vt_ref[c, g], p)
        return carry

    _for_pairs(nck, attn_chunk)

    @pl.when(jnp.min(acc_ref[:, HD_B:HD_B + 1, :]) < _MIN_MASS)
    def _():
        m_ref[...] = jnp.full(m_ref.shape, NEG_BIG, F32)
        acc_ref[...] = jnp.zeros(acc_ref.shape, F32)

        def exact_chunk(c, carry):
            bias, kch = chunk_inputs(c)
            for g in range(N_KV_B):
                s = _dot(kch, qpad_ref[g, 0:N_KV_B * HD_B, :]) + bias
                m_old = m_ref[g]
                m_new = jnp.maximum(m_old, jnp.max(s, axis=0, keepdims=True))
                alpha = jnp.exp2(m_old - m_new)
                p = jnp.exp2(s - m_new).astype(BF16)
                acc_ref[g] = alpha * acc_ref[g] + _dot(vt_ref[c, g], p)
                m_ref[g] = m_new
            return carry

        lax.fori_loop(0, nck, exact_chunk, 0)

    for g in range(N_KV_B):
        acc = acc_ref[g]
        o = acc[:HD_B] / acc[HD_B:HD_B + 1]
        for hp in range(ng // 2):
            pair = jnp.concatenate([o[:, (2 * hp) * tq:(2 * hp + 1) * tq],
                                    o[:, (2 * hp + 1) * tq:(2 * hp + 2) * tq]], axis=0)
            lane0 = (g * ng + 2 * hp) * HD_B
            o_ref[:, lane0:lane0 + 2 * HD_B] = pair.T.astype(BF16)


def _dsa_prompt(qit, wit, kib, qt, kb, vt, *, b, t, tq):
    hd = qt.shape[0]
    nt = t // tq
    ktop = min(TOPK_MAX, t // 4)
    idx_bits = max(1, (t - 1).bit_length())
    ng = N_HEADS_B // N_KV_B
    vrows = vt.shape[2]
    col = lambda bi, i: (0, bi * nt + i)
    return pl.pallas_call(
        functools.partial(_dsa_prompt_body, ktop=ktop, idx_bits=idx_bits),
        grid=(b, nt),
        in_specs=[
            pl.BlockSpec((hd, tq), col),
            pl.BlockSpec((N_IDX_HEADS, tq), col),
            pl.BlockSpec((t, IDX_DIM), lambda bi, i: (bi, 0)),
            pl.BlockSpec((hd, tq), col),
            pl.BlockSpec((t, N_KV_B * HD_B), lambda bi, i: (bi, 0)),
            pl.BlockSpec((nt, N_KV_B, vrows, tq), lambda bi, i: (bi, 0, 0, 0)),
        ],
        out_specs=pl.BlockSpec((tq, hd), lambda bi, i: (bi * nt + i, 0)),
        out_shape=jax.ShapeDtypeStruct((b * t, hd), BF16),
        scratch_shapes=[pltpu.VMEM((t + tq, tq), I32),
                        pltpu.VMEM((N_KV_B, N_KV_B * HD_B + 2 * SUBLANES, ng * tq), BF16),
                        pltpu.VMEM((N_KV_B, 1, ng * tq), F32),
                        pltpu.VMEM((N_KV_B, vrows, ng * tq), F32),
                        pltpu.VMEM((N_KV_B, LANES), F32)],
        compiler_params=_cparams(("parallel", "arbitrary")), name="dsa_prompt",
    )(qit, wit, kib, qt, kb, vt)


_TP = SUBLANES
_NB = LANES
_PG_MAX = 16


def _dsa_sample_body(pt_ref, qi_ref, w_ref, kin_ref, q_ref, kn_ref, vn_ref,
                     ci_hbm, ck_hbm, cv_hbm, o_ref,
                     ibuf, kbuf, vbuf, key_ref, keyn_ref, isem, ksem, vsem,
                     *, n_pages, pg, t_new, ktop, idx_bits):
    b = pl.program_id(0)
    slot = b % 2
    ck = pg * PAGE_SIZE
    nch = n_pages // pg
    past = n_pages * PAGE_SIZE

    def idx_copy(sl, p, page):
        return pltpu.make_async_copy(ci_hbm.at[page], ibuf.at[sl, p], isem.at[sl])

    def k_copy(p, page):
        return pltpu.make_async_copy(ck_hbm.at[page], kbuf.at[p], ksem)

    def v_copy(p, page):
        return pltpu.make_async_copy(cv_hbm.at[page], vbuf.at[p], vsem)

    def issue_idx(bb, sl):
        def f(p, carry):
            idx_copy(sl, p, pt_ref[bb, p]).start()
            return carry

        lax.fori_loop(0, n_pages, f, 0)

    @pl.when(b == 0)
    def _():
        issue_idx(0, 0)

    def issue_kv(p, carry):
        page = pt_ref[b, p]
        k_copy(p, page).start()
        v_copy(p, page).start()
        return carry

    lax.fori_loop(0, n_pages, issue_kv, 0)

    @pl.when(b + 1 < pl.num_programs(0))
    def _():
        issue_idx(b + 1, 1 - slot)

    def wait_pages(hbm, buf, sem):
        pltpu.make_async_copy(hbm.at[pl.ds(0, n_pages)], buf, sem).wait()

    lane = lax.broadcasted_iota(I32, (_TP, ck), 1)
    lane_n = lax.broadcasted_iota(I32, (_TP, _NB), 1)
    row_n = lax.broadcasted_iota(I32, (_TP, _NB), 0)

    qi = qi_ref[0]
    nt = qi.shape[0] // N_IDX_HEADS

    def scores(kpt):
        n = kpt.shape[1]
        s = jnp.maximum(_dot(qi, kpt), 0.0) * w_ref[0]
        return jnp.sum(s.reshape(N_IDX_HEADS, nt, n), axis=0)

    def pages_t(buf, first):
        return jnp.concatenate([buf[first + p] for p in range(pg)], axis=1).astype(BF16)

    wait_pages(ci_hbm, ibuf.at[slot], isem.at[slot])

    real_row = lax.broadcasted_iota(I32, (_TP, ck), 0) < t_new

    def sc_chunk(c, carry):
        sc = scores(pages_t(ibuf.at[slot], c * pg))
        key_ref[c] = _sortable_key(jnp.where(real_row, sc, -jnp.inf))
        return carry

    lax.fori_loop(0, nch, sc_chunk, 0)
    admissible = jnp.logical_and(lane_n <= row_n, row_n < t_new)
    keyn_ref[...] = _sortable_key(jnp.where(admissible, scores(kin_ref[0]), -jnp.inf))

    def count(pred):
        acc = jnp.zeros((_TP, ck), I32)
        for c in range(nch):
            acc = acc + pred(key_ref[c], lane + c * ck).astype(I32)
        acc_n = pred(keyn_ref[...], lane_n + past).astype(I32)
        return jnp.sum(acc, axis=1, keepdims=True) + jnp.sum(acc_n, axis=1, keepdims=True)

    def bit_step(j, carry):
        lo, cnt_lo = carry
        cand = lo + jnp.left_shift(jnp.int32(1), 31 - j)
        cnt = count(lambda blk, _: blk >= cand)
        keep = cnt >= ktop
        return jnp.where(keep, cand, lo), jnp.where(keep, cnt, cnt_lo)

    thr, cnt_ge = lax.fori_loop(
        0, 32, bit_step, (jnp.full((_TP, 1), INT_MIN, I32), jnp.full((_TP, 1), ktop, I32)))
    tie = jnp.logical_and(thr > KEY_NEG_INF, cnt_ge > ktop)

    @pl.when(jnp.max(tie.astype(I32)) > 0)
    def _():
        need = ktop - count(lambda blk, _: blk > thr)

        def idx_step(j, res):
            cand = res + jnp.left_shift(jnp.int32(1), idx_bits - 1 - j)
            cnt = count(lambda blk, kidx: jnp.logical_and(blk == thr, kidx < cand))
            return jnp.where(cnt < need, cand, res)

        jcut = lax.fori_loop(0, idx_bits, idx_step, jnp.zeros((_TP, 1), I32))

        def dropped(blk, kidx):
            return jnp.where(jnp.logical_and(blk == thr, kidx > jcut), blk - 1, blk)

        def drop_chunk(c, carry):
            key_ref[c] = dropped(key_ref[c], lane + c * ck)
            return carry

        lax.fori_loop(0, nch, drop_chunk, 0)
        keyn_ref[...] = dropped(keyn_ref[...], lane_n + past)

    thr_eff = jnp.maximum(thr, KEY_NEG_INF + 1)
    qp = q_ref[0]
    reps = qp.shape[0] // _TP

    def attend(keys, kct, vct, carry):
        m, l, acc = carry
        bias = jnp.where(keys >= thr_eff, 0.0, NEG_BIG)
        s = _dot(qp, kct) + jnp.concatenate([bias] * reps, axis=0)
        m_new = jnp.maximum(m, jnp.max(s, axis=-1, keepdims=True))
        alpha = jnp.exp2(m - m_new)
        p = jnp.exp2(s - m_new)
        l = alpha * l + jnp.sum(p, axis=-1, keepdims=True)
        acc = alpha * acc + _dot_nt(p.astype(BF16), vct)
        return m_new, l, acc

    wait_pages(ck_hbm, kbuf, ksem)
    wait_pages(cv_hbm, vbuf, vsem)

    def at_chunk(c, carry):
        return attend(key_ref[c], pages_t(kbuf, c * pg), pages_t(vbuf, c * pg), carry)

    nr = qp.shape[0]
    init = (jnp.full((nr, 1), NEG_BIG, F32), jnp.zeros((nr, 1), F32),
            jnp.zeros((nr, N_KV_B * HD_B), F32))
    carry = lax.fori_loop(0, nch, at_chunk, init)
    m, l, acc = attend(keyn_ref[...], kn_ref[0], vn_ref[0], carry)
    o_ref[0] = acc / l


def _dsa_sample(page_table, qi, w, kin, qp, kn, vn, cache_i, cache_k, cache_v, *, t_new):
    b, n_pages = page_table.shape
    pg = math.gcd(n_pages, _PG_MAX)
    nr = qp.shape[1]
    total = n_pages * PAGE_SIZE + t_new
    ktop = min(TOPK_MAX, total // 4)
    idx_bits = max(1, (total - 1).bit_length())
    nch = n_pages // pg
    per_b = lambda i, pt: (i, 0, 0)
    grid_spec = pltpu.PrefetchScalarGridSpec(
        num_scalar_prefetch=1, grid=(b,),
        in_specs=[pl.BlockSpec((1,) + qi.shape[1:], per_b), pl.BlockSpec((1,) + w.shape[1:], per_b),
                  pl.BlockSpec((1,) + kin.shape[1:], per_b), pl.BlockSpec((1,) + qp.shape[1:], per_b),
                  pl.BlockSpec((1,) + kn.shape[1:], per_b), pl.BlockSpec((1,) + vn.shape[1:], per_b),
                  pl.BlockSpec(memory_space=pl.ANY), pl.BlockSpec(memory_space=pl.ANY),
                  pl.BlockSpec(memory_space=pl.ANY)],
        out_specs=pl.BlockSpec((1, nr, N_KV_B * HD_B), per_b),
        scratch_shapes=[pltpu.VMEM((2, n_pages, IDX_DIM, PAGE_SIZE), F32),
                        pltpu.VMEM((n_pages, N_KV_B * HD_B, PAGE_SIZE), F32),
                        pltpu.VMEM((n_pages, N_KV_B * HD_B, PAGE_SIZE), F32),
                        pltpu.VMEM((nch, _TP, pg * PAGE_SIZE), I32),
                        pltpu.VMEM((_TP, _NB), I32),
                        pltpu.SemaphoreType.DMA((2,)),
                        pltpu.SemaphoreType.DMA(()),
                        pltpu.SemaphoreType.DMA(())])
    return pl.pallas_call(
        functools.partial(_dsa_sample_body, n_pages=n_pages, pg=pg, t_new=t_new, ktop=ktop,
                          idx_bits=idx_bits),
        grid_spec=grid_spec,
        out_shape=jax.ShapeDtypeStruct((b, nr, N_KV_B * HD_B), F32),
        compiler_params=_cparams(("arbitrary",)), name="dsa_sample",
    )(page_table, qi, w, kin, qp, kn, vn, cache_i, cache_k, cache_v)


def _log_sigmoid(x):
    return jnp.minimum(x, 0.0) - jnp.log1p(jnp.exp(-jnp.abs(x)))


def _odd_proj_body(x_ref, g_ref, w_ref, wt_ref, wg_ref, bg_ref, bgt_ref,
                   qt_ref, k_ref, vt_ref, so_ref, gate_ref, gatet_ref):
    xn = _rms(x_ref[...], g_ref[...]).astype(BF16)
    nqk = N_HEADS_C * DQK_C
    k_ref[...] = _dot(xn, w_ref[:, :nqk]).astype(BF16)
    so_ref[...] = jax.nn.sigmoid(_dot(xn, w_ref[:, nqk:])).astype(BF16)
    nv = N_HEADS_C * DV_C
    allt = _dot_nt(wt_ref[...], xn)
    qt_ref[...] = (allt[:nqk] * DQK_C ** -0.5).astype(BF16)
    vt_ref[...] = allt[nqk:nqk + nv].astype(BF16)
    nh = N_HEADS_C
    gate = _dot(xn, wg_ref[...]) + bg_ref[...]
    col = lax.broadcasted_iota(I32, gate.shape, 1)
    gate_ref[...] = jnp.where(col < nh, gate, _log_sigmoid(gate))
    gatet = allt[nqk + nv:] + bgt_ref[...]
    rowi = lax.broadcasted_iota(I32, gatet.shape, 0)
    gatet_ref[...] = jnp.where(rowi < nh, gatet, _log_sigmoid(gatet))


def _odd_proj(x, g, w, wt, wg, bg, bgt, *, tm):
    m = x.shape[0]
    row = lambda i: (i, 0)
    col = lambda i: (0, i)
    const = lambda i: (0, 0)
    nqk = N_HEADS_C * DQK_C
    nv = N_HEADS_C * DV_C
    ng = 2 * N_HEADS_C
    return pl.pallas_call(
        _odd_proj_body, grid=(m // tm,),
        in_specs=[pl.BlockSpec((tm, D_MODEL), row), pl.BlockSpec((1, D_MODEL), const),
                  pl.BlockSpec(w.shape, const), pl.BlockSpec(wt.shape, const),
                  pl.BlockSpec((D_MODEL, ng), const), pl.BlockSpec((1, ng), const),
                  pl.BlockSpec((ng, 1), const)],
        out_specs=(pl.BlockSpec((nqk, tm), col), pl.BlockSpec((tm, nqk), row),
                   pl.BlockSpec((nv, tm), col), pl.BlockSpec((tm, nv), row),
                   pl.BlockSpec((tm, ng), row), pl.BlockSpec((ng, tm), col)),
        out_shape=(jax.ShapeDtypeStruct((nqk, m), BF16), jax.ShapeDtypeStruct((m, nqk), BF16),
                   jax.ShapeDtypeStruct((nv, m), BF16), jax.ShapeDtypeStruct((m, nv), BF16),
                   jax.ShapeDtypeStruct((m, ng), F32), jax.ShapeDtypeStruct((ng, m), F32)),
        compiler_params=_cparams(("parallel",)), name="odd_proj",
    )(x, g, w, wt, wg, bg, bgt)


def _split3(x):
    hi = x.astype(BF16)
    r1 = x - hi.astype(F32)
    mid = r1.astype(BF16)
    lo = (r1 - mid.astype(F32)).astype(BF16)
    return hi, mid, lo


def _mlstm_body(qt_ref, k_ref, vt_ref, so_ref, gate_ref, gatet_ref, hg_ref,
                c0_ref, n0_ref, m0_ref,
                h_ref, c_ref, n_ref, m_ref, cs_ref, ns_ref, ms_ref):
    lc = k_ref.shape[0]
    nh = N_HEADS_C
    c = pl.program_id(1)

    @pl.when(c == 0)
    def _():
        cs_ref[...] = c0_ref[0]
        ns_ref[...] = n0_ref[0]
        ms_ref[...] = jnp.broadcast_to(m0_ref[0], ms_ref.shape)

    ri = lax.broadcasted_iota(I32, (lc, lc), 0)
    ci = lax.broadcasted_iota(I32, (lc, lc), 1)
    causal_t = ri <= ci
    tril = (ci <= ri).astype(BF16)
    triu = causal_t.astype(BF16)
    gate = gate_ref[...]
    gatet = gatet_ref[...]
    lf3 = _split3(gate[:, nh:])
    bcols = _dot(tril, lf3[0]) + _dot(tril, lf3[1]) + _dot(tril, lf3[2])
    lft3 = _split3(gatet[nh:, :])
    brows = _dot(lft3[0], triu) + _dot(lft3[1], triu) + _dot(lft3[2], triu)
    half_lane = lax.broadcasted_iota(I32, (1, 2 * DQK_C), 1) // DQK_C
    zero_q = jnp.zeros((DQK_C, lc), BF16)
    ms = ms_ref[...]
    ms_new = []
    for h in range(nh):
        j, half = divmod(h, 2)
        in_half = half_lane == half
        kp = k_ref[:, j * 2 * DQK_C:(j + 1) * 2 * DQK_C]
        qh = qt_ref[h * DQK_C:(h + 1) * DQK_C, :]
        qpad = jnp.concatenate([qh, zero_q] if half == 0 else [zero_q, qh], axis=0)
        vt = vt_ref[h * DV_C:(h + 1) * DV_C, :]
        br = brows[h:h + 1, :]
        igr = gatet[h:h + 1, :]
        m_prev = ms[h:h + 1, 0:1]
        a = br + m_prev
        src = gate[:, h:h + 1] - bcols[:, h:h + 1]
        d = jnp.where(causal_t, br + src, NEG_BIG)
        mj = jnp.maximum(a, jnp.max(d, axis=0, keepdims=True))
        s = _dot(kp, qpad) * jnp.exp(d - mj)
        aw = jnp.exp(a - mj)
        cp = cs_ref[j]
        n8 = jnp.broadcast_to(ns_ref[j:j + 1, :], (SUBLANES, 2 * DQK_C)).astype(BF16)
        num = _dot(vt, s.astype(BF16)) + aw * _dot(cp.astype(BF16), qpad)
        den = jnp.sum(s, axis=0, keepdims=True) + aw * _dot(n8, qpad)[0:1]
        ht = num / jnp.maximum(jnp.abs(den), jnp.exp(-mj))
        ht = ht * lax.rsqrt(jnp.mean(ht * ht, axis=0, keepdims=True) + NORM_EPS)
        sl = slice(h * DV_C, (h + 1) * DV_C)
        h_ref[:, sl] = (ht.T * hg_ref[:, sl] * so_ref[:, sl].astype(F32)).astype(BF16)
        b_last = br[:, lc - 1:lc]
        g_row = b_last - br + igr
        m_new = jnp.maximum(b_last + m_prev, jnp.max(g_row, axis=-1, keepdims=True))
        gw = jnp.exp(g_row - m_new)
        decay = jnp.exp(b_last + m_prev - m_new)
        upd = _dot((vt.astype(F32) * gw).astype(BF16), kp)
        cs_ref[j] = jnp.where(in_half, decay * cp + upd, cp)
        gw8 = jnp.broadcast_to(gw, (SUBLANES, lc)).astype(BF16)
        n_old = ns_ref[j:j + 1, :]
        ns_ref[j:j + 1, :] = jnp.where(in_half, decay * n_old + _dot(gw8, kp)[0:1], n_old)
        ms_new.append(jnp.broadcast_to(m_new, (1, ms.shape[1])))
    ms_ref[...] = jnp.concatenate(ms_new, axis=0)

    @pl.when(c == pl.num_programs(1) - 1)
    def _():
        c_ref[0] = cs_ref[...]
        n_ref[0] = ns_ref[...]
        m_ref[0] = ms_ref[...]


def _mlstm(qt, k, vt, so, gate, gatet, hg, c0p, n0p, m0, *, b, t, lc):
    nh = N_HEADS_C
    nqk = nh * DQK_C
    nv = nh * DV_C
    ng = gate.shape[1]
    nc = t // lc
    npair = nh // 2
    row = lambda bi, c: (bi * nc + c, 0)
    col = lambda bi, c: (0, bi * nc + c)
    per_b4 = lambda bi, c: (bi, 0, 0, 0)
    per_b3 = lambda bi, c: (bi, 0, 0)
    return pl.pallas_call(
        _mlstm_body, grid=(b, nc),
        in_specs=[pl.BlockSpec((nqk, lc), col), pl.BlockSpec((lc, nqk), row),
                  pl.BlockSpec((nv, lc), col), pl.BlockSpec((lc, nv), row),
                  pl.BlockSpec((lc, ng), row), pl.BlockSpec((ng, lc), col),
                  pl.BlockSpec((1, nv), lambda bi, c: (0, 0)),
                  pl.BlockSpec((1, npair, DV_C, 2 * DQK_C), per_b4),
                  pl.BlockSpec((1, npair, 2 * DQK_C), per_b3),
                  pl.BlockSpec((1, nh, 1), per_b3)],
        out_specs=(pl.BlockSpec((lc, nv), row),
                   pl.BlockSpec((1, npair, DV_C, 2 * DQK_C), per_b4),
                   pl.BlockSpec((1, npair, 2 * DQK_C), per_b3),
                   pl.BlockSpec((1, nh, LANES), per_b3)),
        out_shape=(jax.ShapeDtypeStruct((b * t, nv), BF16),
                   jax.ShapeDtypeStruct((b, npair, DV_C, 2 * DQK_C), F32),
                   jax.ShapeDtypeStruct((b, npair, 2 * DQK_C), F32),
                   jax.ShapeDtypeStruct((b, nh, LANES), F32)),
        scratch_shapes=[pltpu.VMEM((npair, DV_C, 2 * DQK_C), F32),
                        pltpu.VMEM((npair, 2 * DQK_C), F32), pltpu.VMEM((nh, LANES), F32)],
        compiler_params=_cparams(("parallel", "arbitrary")), name="mlstm",
    )(qt, k, vt, so, gate, gatet, hg, c0p, n0p, m0)


def _prep_even(w_in, w_out):
    sizes = [D_CONV, D_CONV, D_CONV, N_HEADS_B * HD_B, N_KV_B * HD_B, N_KV_B * HD_B,
             N_IDX_HEADS * IDX_DIM, N_IDX_HEADS]
    offs = [sum(sizes[:j + 1]) for j in range(len(sizes))]
    u, gb, gc, q, k, v, qi, wi, ki = jnp.split(w_in, offs, axis=1)
    pad = jnp.zeros((D_MODEL, LANES - IDX_DIM), w_in.dtype)
    w = jnp.concatenate([u, gb, gc, q, k, v, qi, ki, pad], axis=1).astype(BF16)
    wpad = jnp.zeros((D_MODEL, 2 * SUBLANES - N_IDX_HEADS), w_in.dtype)
    wt = jnp.concatenate([q, qi, v, wi, wpad], axis=1).T.astype(BF16)
    return (w, wt, wi.T.astype(BF16), w_out[:D_CONV].astype(BF16), w_out[D_CONV:].astype(BF16))


def _prep_odd(w_in, b_i, b_f):
    q, k, v, o, wg = jnp.split(w_in, [_OD_K, _OD_V, _OD_O, _OD_END], axis=1)
    w = jnp.concatenate([k, o], axis=1).astype(BF16)
    wt = jnp.concatenate([q, v, wg], axis=1).T.astype(BF16)
    bg = jnp.concatenate([b_i, b_f])
    return w, wt, wg.astype(BF16), bg[None, :], bg[:, None]


def _tile(n, pref):
    return pref if n % pref == 0 else n


def _pad_axis(a, axis, n):
    if a.shape[axis] == n:
        return a
    widths = [(0, 0)] * a.ndim
    widths[axis] = (0, n - a.shape[axis])
    return jnp.pad(a, widths)


def _tail(x2, qx, mk, mv, wo, g3, g4, g5, w1, w2, *, b, t):
    t_pad = max(t, 2 * SUBLANES)
    tx = _tile(t_pad, 512)
    q3 = _pad_axis(qx.reshape(b, t, -1), 1, t_pad)
    x3 = _pad_axis(x2.reshape(b, t, D_MODEL), 1, t_pad)
    bb = 1 if tx >= LANES else math.gcd(b, LANES // tx)
    x3 = _xattn(q3, mk, mv, x3, wo, g3, tm=tx, bb=bb)[:, :t].reshape(b * t, D_MODEL)
    return _mlp(x3, g4, g5, w1, w2, tm=_tile(b * t, 512))


def _even_prompt(x2, g, we, cw, wxq, *, b, t, tm):
    w, wt, wwi, wo_a, wo_b = we
    tq = _tile(t, 256)
    tabs = _rope_tables(jnp.arange(t))
    tabs_t = tuple(tab.T for tab in tabs)
    init = jnp.zeros((b, CONV_W - 1, D_CONV), F32)
    ya, k, v, ki, kb, kib, qt, qit, vt, wit, conv = _even_proj_prompt(
        x2, g[0:1], w, wt, cw, tabs, tabs_t, init, tm=tm, seq_len=t, kc=tq)
    yb = _dsa_prompt(qit, wit, kib, qt, kb, vt, b=b, t=t, tq=tq)
    x1, qx = _mix_out([ya, yb], [wo_a, wo_b], x2, g[1:2], g[2:3], wxq, tm=tm)
    state = (k.reshape(b, t, N_KV_B, HD_B), v.reshape(b, t, N_KV_B, HD_B),
             ki.reshape(b, t, IDX_DIM), conv)
    return x1, qx, state


def _odd_mixer(x2, g, wod, hg, w_out, wxq, c0, n0, m0, *, b, t, tm, lc):
    w, wt, wg, bg, bgt = wod
    qt, k, vt, so, gate, gatet = _odd_proj(x2, g[0:1], w, wt, wg, bg, bgt, tm=tm)
    nh = N_HEADS_C
    tp = -(-t // lc) * lc
    if tp > t:
        rows = lambda a: _pad_axis(a.reshape(b, t, -1), 1, tp).reshape(b * tp, -1)
        cols = lambda a: _pad_axis(a.reshape(-1, b, t), 2, tp).reshape(-1, b * tp)
        neutral = jnp.concatenate([jnp.full((nh,), NEG_BIG, F32), jnp.zeros((nh,), F32)])
        gate = jnp.concatenate(
            [gate.reshape(b, t, 2 * nh),
             jnp.broadcast_to(neutral[None, None, :], (b, tp - t, 2 * nh))], axis=1)
        gatet = jnp.concatenate(
            [gatet.reshape(2 * nh, b, t),
             jnp.broadcast_to(neutral[:, None, None], (2 * nh, b, tp - t))], axis=2)
        qt, k, vt, so = cols(qt), rows(k), cols(vt), rows(so)
        gate, gatet = gate.reshape(b * tp, 2 * nh), gatet.reshape(2 * nh, b * tp)
    c0p = c0.reshape(b, nh // 2, 2, DV_C, DQK_C).transpose(0, 1, 3, 2, 4)
    c0p = c0p.reshape(b, nh // 2, DV_C, 2 * DQK_C)
    h, cp, n, m = _mlstm(qt, k, vt, so, gate, gatet, hg, c0p, n0.reshape(b, nh // 2, 2 * DQK_C),
                         m0[:, :, None], b=b, t=tp, lc=lc)
    h2 = h.reshape(b, tp, nh * DV_C)[:, :t].reshape(b * t, nh * DV_C)
    x1, qx = _mix_out([h2], [w_out], x2, g[1:2], g[2:3], wxq, tm=tm)
    c_out = cp.reshape(b, nh // 2, DV_C, 2, DQK_C).transpose(0, 1, 3, 2, 4)
    c_out = c_out.reshape(b, nh, DV_C, DQK_C)
    return x1, qx, (c_out, n.reshape(b, nh, DQK_C), m[:, :, 0])


def _even_sample(x2, g, we, cw, wxq, cache_i, cache_k, cache_v, state_conv, page_table, *, b, t):
    w, _, wwi, wo_a, wo_b = we
    n_pages = page_table.shape[1]
    pos = n_pages * PAGE_SIZE + jnp.arange(t)
    tabs = tuple(jnp.repeat(tab, b, axis=0) for tab in _rope_tables(pos))
    to_tm = lambda a: a.reshape(b, t, -1).transpose(1, 0, 2).reshape(t * b, -1)
    to_bm = lambda a: a.reshape(t, b, -1).transpose(1, 0, 2)
    init = state_conv.transpose(1, 0, 2).reshape(2 * b, D_CONV)
    ya, q, k, v, qi, ki, wit, conv = _even_proj(
        to_tm(x2), g[0:1], w, wwi, cw, tabs, init, seq_len=t, time_major_b=b)
    ng = N_HEADS_B // N_KV_B
    qi_b = to_bm(qi).reshape(b, t, N_IDX_HEADS, IDX_DIM).transpose(0, 2, 1, 3)
    qi_b = _pad_axis(qi_b, 2, _TP).reshape(b, N_IDX_HEADS * _TP, IDX_DIM)
    w_b = _pad_axis(wit.reshape(N_IDX_HEADS, t, b).transpose(2, 0, 1), 2, _TP)
    w_b = w_b.reshape(b, N_IDX_HEADS * _TP, 1)
    new_t = lambda a: _pad_axis(to_bm(a), 1, _NB).transpose(0, 2, 1).astype(BF16)
    kin, kn, vn = new_t(ki), new_t(k), new_t(v)
    q5 = _pad_axis(to_bm(q).reshape(b, t, N_KV_B, ng, HD_B).transpose(0, 2, 3, 1, 4), 3, _TP)
    zq = jnp.zeros_like(q5[:, 0])
    qp = jnp.stack([jnp.concatenate([q5[:, 0], zq], axis=-1),
                    jnp.concatenate([zq, q5[:, 1]], axis=-1)], axis=1)
    qp = qp.reshape(b, N_KV_B * ng * _TP, N_KV_B * HD_B)
    n_phys = cache_k.shape[0]
    pool_t = lambda a: a.reshape(n_phys, PAGE_SIZE, -1).transpose(0, 2, 1)
    o = _dsa_sample(page_table, qi_b, w_b, kin, qp, kn, vn,
                    pool_t(cache_i), pool_t(cache_k), pool_t(cache_v), t_new=t)
    o = o.reshape(b, N_KV_B, ng, _TP, N_KV_B, HD_B)[:, :, :, :t]
    yb = jnp.stack([o[:, 0, :, :, 0], o[:, 1, :, :, 1]], axis=1)
    yb = yb.transpose(3, 0, 1, 2, 4).reshape(t * b, N_HEADS_B * HD_B).astype(BF16)
    x1, qx = _mix_out([ya, yb], [wo_a, wo_b], to_tm(x2), g[1:2], g[2:3], wxq, tm=t * b)
    back = lambda a: to_bm(a).reshape(b * t, -1)
    state = (to_bm(k).reshape(b, t, N_KV_B, HD_B), to_bm(v).reshape(b, t, N_KV_B, HD_B),
             to_bm(ki), conv.reshape(2, b, D_CONV).transpose(1, 0, 2))
    return back(x1), back(qx), state


def kernel(x_prompt, x_sample, cache_k, cache_v, cache_idx_k, cache_mem_k, cache_mem_v, state_conv, state_C, state_n, state_m, page_table, mem_prompt, norm_g, w_in_even, conv_w, w_out_even, w_in_odd, b_i, b_f, hnorm_g, w_out_odd, mem_norm_g, w_xq, w_xk, w_xv, w_xo, w_ff1, w_ff2):
    bp, tp, _ = x_prompt.shape
    bs, ts, _ = x_sample.shape
    depth = norm_g.shape[0]
    tm_p = _tile(tp, 512)
    lc = _tile(tp, 256)
    nmem = mem_prompt.shape[1]
    mk_all, mv_all = _mem_kv(mem_prompt.reshape(bp * nmem, D_MODEL), mem_norm_g[:, None, :],
                             w_xk.astype(BF16), w_xv.astype(BF16), tm=_tile(bp * nmem, 512))
    xp = x_prompt.reshape(bp * tp, D_MODEL)
    xs = x_sample.reshape(bs * ts, D_MODEL)
    ev_p, ev_s, od_p, od_s = [], [], [], []
    for l in range(depth):
        g = norm_g[l]
        wxq = w_xq[l].astype(BF16)
        if l % 2 == 0:
            e = l // 2
            we = _prep_even(w_in_even[e], w_out_even[e])
            xp, qxp, st = _even_prompt(xp, g, we, conv_w[e], wxq, b=bp, t=tp, tm=tm_p)
            ev_p.append(st)
            xs, qxs, st = _even_sample(xs, g, we, conv_w[e], wxq, cache_idx_k[e], cache_k[e],
                                       cache_v[e], state_conv[e], page_table, b=bs, t=ts)
            ev_s.append(st)
        else:
            o = l // 2
            wod = _prep_odd(w_in_odd[o], b_i[o], b_f[o])
            hg = hnorm_g[o][None, :]
            wout = w_out_odd[o].astype(BF16)
            zc = jnp.zeros((bp, N_HEADS_C, DV_C, DQK_C), F32)
            zn = jnp.zeros((bp, N_HEADS_C, DQK_C), F32)
            zm = jnp.zeros((bp, N_HEADS_C), F32)
            xp, qxp, st = _odd_mixer(xp, g, wod, hg, wout, wxq, zc, zn, zm, b=bp, t=tp, tm=tm_p, lc=lc)
            od_p.append(st)
            xs, qxs, st = _odd_mixer(xs, g, wod, hg, wout, wxq, state_C[o], state_n[o], state_m[o],
                                     b=bs, t=ts, tm=bs * ts, lc=LANES)
            od_s.append(st)
        wxo = w_xo[l].astype(BF16)
        w1 = w_ff1[l].astype(BF16)
        w2 = w_ff2[l].astype(BF16)
        nx = N_HEADS_X * HD_X
        xp = _tail(xp, qxp, mk_all[l].reshape(bp, nmem, nx), mv_all[l].reshape(bp, nmem, nx),
                   wxo, g[3:4], g[4:5], g[5:6], w1, w2, b=bp, t=tp)
        xs = _tail(xs, qxs, cache_mem_k[l].reshape(bs, nmem, nx), cache_mem_v[l].reshape(bs, nmem, nx),
                   wxo, g[3:4], g[4:5], g[5:6], w1, w2, b=bs, t=ts)
    stack = lambda sts, j: jnp.stack([s[j] for s in sts])
    mem_shape = (depth, bp, nmem, N_HEADS_X, HD_X)
    return (xp.reshape(bp, tp, D_MODEL), xs.reshape(bs, ts, D_MODEL),
            stack(ev_p, 0), stack(ev_p, 1), stack(ev_p, 2), stack(ev_p, 3),
            stack(od_p, 0), stack(od_p, 1), stack(od_p, 2),
            mk_all.reshape(mem_shape), mv_all.reshape(mem_shape),
            stack(ev_s, 0), stack(ev_s, 1), stack(ev_s, 2), stack(ev_s, 3),
            stack(od_s, 0), stack(od_s, 1), stack(od_s, 2))
```

```python
import functools
import math

import jax
import jax.numpy as jnp
from jax import lax
from jax.experimental import pallas as pl
from jax.experimental.pallas import tpu as pltpu

F32 = jnp.float32
BF16 = jnp.bfloat16
I32 = jnp.int32

D_MODEL = 1024
D_CONV = 512
CONV_W = 3
N_HEADS_B = 8
N_KV_B = 2
HD_B = 64
N_IDX_HEADS = 8
IDX_DIM = 64
TOPK_MAX = 256
N_HEADS_C = 8
DQK_C = 64
DV_C = 128
N_MEM = 256
N_HEADS_X = 4
HD_X = 128
D_FF = 4096
PAGE_SIZE = 128
ROPE_THETA = 500000.0
NORM_EPS = 1e-6

LANES = 128
SUBLANES = 8
VMEM_LIMIT = 48 * 1024 * 1024

NEG_BIG = -1e30
INT_MIN = -(2 ** 31)
KEY_NEG_INF = -(2 ** 31) + 0x007FFFFF

_EV_U, _EV_GB, _EV_GC, _EV_Q, _EV_K, _EV_V, _EV_QI, _EV_KI, _EV_END = (
    0, 512, 1024, 1536, 2048, 2176, 2304, 2816, 2944)
_OD_Q, _OD_K, _OD_V, _OD_O, _OD_END = 0, 512, 1024, 2048, 3072


def _cparams(sem, vmem=VMEM_LIMIT):
    return pltpu.CompilerParams(dimension_semantics=sem, vmem_limit_bytes=vmem)


def _rms(x, g):
    return x * lax.rsqrt(jnp.mean(x * x, axis=-1, keepdims=True) + NORM_EPS) * g


def _dot(a, b):
    return jnp.dot(a, b, preferred_element_type=F32)


def _dot_nt(a, b):
    return lax.dot_general(a, b, (((1,), (1,)), ((), ())), preferred_element_type=F32)


def _rope128(x, c, sa, sb):
    return x * c + pltpu.roll(x, LANES - 8, 1) * sa + pltpu.roll(x, 8, 1) * sb


def _rope_tables(pos):
    r = HD_B // 4
    half = r // 2
    freqs = ROPE_THETA ** (-jnp.arange(half, dtype=F32) * 2.0 / r)
    ang = pos.astype(F32)[:, None] * freqs[None, :]
    cos, sin = jnp.cos(ang), jnp.sin(ang)
    t = pos.shape[0]
    ones = jnp.ones((t, HD_B - r), F32)
    zeros = jnp.zeros((t, HD_B - r), F32)
    zh = jnp.zeros((t, half), F32)
    c = jnp.concatenate([cos, cos, ones], axis=1)
    sa = jnp.concatenate([-sin, zh, zeros], axis=1)
    sb = jnp.concatenate([zh, sin, zeros], axis=1)
    tile2 = lambda a: jnp.concatenate([a, a], axis=1)
    return tile2(c), tile2(sa), tile2(sb)


_Q_SCALE = HD_B ** -0.5 * math.log2(math.e)
_ONES_ROWS = 16
_BOUND_SLACK = 1.01
_MIN_MASS = 2.0 ** -100
_HI16_MASK = -(2 ** 16)
_MIN_NORMAL16 = 0x0080


def _rope128_t(x, c, sa, sb):
    return x * c + pltpu.roll(x, LANES - 8, 0) * sa + pltpu.roll(x, 8, 0) * sb


def _even_proj_body(x_ref, g_ref, w_ref, wwi_ref, cw_ref, cos_ref, sa_ref, sb_ref, init_ref,
                    ya_ref, q_ref, k_ref, v_ref, qi_ref, ki_ref, wit_ref, conv_ref, *, time_major_b):
    tm = x_ref.shape[0]
    xn = _rms(x_ref[...], g_ref[...]).astype(BF16)
    c, sa, sb = cos_ref[...], sa_ref[...], sb_ref[...]

    def seg(a, b):
        return _dot(xn, w_ref[:, a:b])

    z = seg(_EV_GC, _EV_Q) * seg(_EV_U, _EV_GB)
    cw = cw_ref[...]
    nb = time_major_b
    init = init_ref[...]
    z1 = jnp.concatenate([init[nb:2 * nb], z[:tm - nb]], axis=0)
    z2 = jnp.concatenate([init, z[:tm - 2 * nb]], axis=0)
    conv_ref[...] = z[tm - 2 * nb:, :]
    conv = cw[0:1] * z2 + cw[1:2] * z1 + cw[2:3] * z
    ya_ref[...] = (seg(_EV_GB, _EV_GC) * conv).astype(BF16)

    for j in range(4):
        a = _EV_Q + j * LANES
        q_ref[:, j * LANES:(j + 1) * LANES] = (
            _rope128(seg(a, a + LANES), c, sa, sb) * _Q_SCALE).astype(BF16)
        a = _EV_QI + j * LANES
        qi_ref[:, j * LANES:(j + 1) * LANES] = (
            _rope128(seg(a, a + LANES), c, sa, sb) * IDX_DIM ** -0.5).astype(BF16)
    k_ref[...] = _rope128(seg(_EV_K, _EV_V), c, sa, sb)
    v_ref[...] = seg(_EV_V, _EV_QI)
    ki_ref[...] = _rope128(seg(_EV_KI, _EV_END), c, sa, sb)[:, :IDX_DIM]
    wit_ref[...] = _dot_nt(wwi_ref[...], xn) * N_IDX_HEADS ** -0.5


def _even_proj_prompt_body(x_ref, g_ref, w_ref, wt_ref, cw_ref, cos_ref, sa_ref, sb_ref,
                           cost_ref, sat_ref, sbt_ref, init_ref,
                           ya_ref, k_ref, v_ref, ki_ref, kb_ref, kib_ref, qt_ref, qit_ref, vt_ref,
                           wit_ref, conv_ref, carry_ref, *, tiles_per_seq):
    tm = x_ref.shape[0]
    xn = _rms(x_ref[...], g_ref[...]).astype(BF16)
    c, sa, sb = cos_ref[...], sa_ref[...], sb_ref[...]
    ct, sat, sbt = cost_ref[...], sat_ref[...], sbt_ref[...]

    def seg(a, b):
        return _dot(xn, w_ref[:, a:b])

    z = seg(_EV_GC, _EV_Q) * seg(_EV_U, _EV_GB)
    cw = cw_ref[...]

    @pl.when(pl.program_id(0) % tiles_per_seq == 0)
    def _():
        carry_ref[0:2, :] = init_ref[0]

    c0 = carry_ref[0:1, :]
    c1 = carry_ref[1:2, :]
    row = lax.broadcasted_iota(I32, (tm, 1), 0)
    z1 = jnp.where(row == 0, c1, pltpu.roll(z, 1, 0))
    z2 = jnp.where(row == 0, c0, jnp.where(row == 1, c1, pltpu.roll(z, 2, 0)))
    carry_ref[0:2, :] = z[tm - 2:tm, :]
    conv_ref[0] = z[tm - 2:tm, :]
    conv = cw[0:1] * z2 + cw[1:2] * z1 + cw[2:3] * z
    ya_ref[...] = (seg(_EV_GB, _EV_GC) * conv).astype(BF16)

    k = _rope128(seg(_EV_K, _EV_V), c, sa, sb)
    k_ref[...] = k
    kb_ref[...] = k.astype(BF16)
    v_ref[...] = seg(_EV_V, _EV_QI)
    ki = _rope128(seg(_EV_KI, _EV_END), c, sa, sb)[:, :IDX_DIM]
    ki_ref[...] = ki
    kib_ref[...] = ki.astype(BF16)

    nq = N_HEADS_B * HD_B
    nqi = N_IDX_HEADS * IDX_DIM
    allt = _dot_nt(wt_ref[...], xn)
    for j in range(nq // LANES):
        xt = allt[j * LANES:(j + 1) * LANES]
        qt_ref[j * LANES:(j + 1) * LANES, :] = (_rope128_t(xt, ct, sat, sbt) * _Q_SCALE).astype(BF16)
    for j in range(nqi // LANES):
        xt = allt[nq + j * LANES:nq + (j + 1) * LANES]
        qit_ref[j * LANES:(j + 1) * LANES, :] = (
            _rope128_t(xt, ct, sat, sbt) * IDX_DIM ** -0.5).astype(BF16)
    nkv = N_KV_B * HD_B
    vt = allt[nq + nqi:nq + nqi + nkv].astype(BF16)
    kc = vt_ref.shape[3]
    ones = jnp.ones((_ONES_ROWS, kc), BF16)
    for j in range(tm // kc):
        for g in range(N_KV_B):
            vt_ref[j, g] = jnp.concatenate(
                [vt[g * HD_B:(g + 1) * HD_B, j * kc:(j + 1) * kc], ones], axis=0)
    wit_ref[...] = allt[nq + nqi + nkv:nq + nqi + nkv + N_IDX_HEADS] * N_IDX_HEADS ** -0.5


def _even_proj_prompt(x, g, w, wt, cw, tabs, tabs_t, init, *, tm, seq_len, kc):
    m = x.shape[0]
    tiles_per_seq = seq_len // tm
    row = lambda i: (i, 0)
    col = lambda i: (0, i)
    const = lambda i: (0, 0)
    seq = lambda i: (i // tiles_per_seq, 0, 0)
    nq = N_HEADS_B * HD_B
    nqi = N_IDX_HEADS * IDX_DIM
    nkv = N_KV_B * HD_B
    tab_spec = pl.BlockSpec((tm, LANES), lambda i: (i % tiles_per_seq, 0))
    tabt_spec = pl.BlockSpec((LANES, tm), lambda i: (0, i % tiles_per_seq))
    vrows = HD_B + _ONES_ROWS
    in_specs = [
        pl.BlockSpec((tm, D_MODEL), row), pl.BlockSpec((1, D_MODEL), const),
        pl.BlockSpec((D_MODEL, _EV_END), const), pl.BlockSpec(wt.shape, const),
        pl.BlockSpec((CONV_W, D_CONV), const),
        tab_spec, tab_spec, tab_spec, tabt_spec, tabt_spec, tabt_spec,
        pl.BlockSpec((1, 2, D_CONV), seq),
    ]
    out_specs = (
        pl.BlockSpec((tm, D_CONV), row),
        pl.BlockSpec((tm, nkv), row), pl.BlockSpec((tm, nkv), row), pl.BlockSpec((tm, IDX_DIM), row),
        pl.BlockSpec((tm, nkv), row), pl.BlockSpec((tm, IDX_DIM), row),
        pl.BlockSpec((nq, tm), col), pl.BlockSpec((nqi, tm), col),
        pl.BlockSpec((tm // kc, N_KV_B, vrows, kc), lambda i: (i, 0, 0, 0)),
        pl.BlockSpec((N_IDX_HEADS, tm), col),
        pl.BlockSpec((1, 2, D_CONV), seq),
    )
    out_shape = (
        jax.ShapeDtypeStruct((m, D_CONV), BF16),
        jax.ShapeDtypeStruct((m, nkv), F32), jax.ShapeDtypeStruct((m, nkv), F32),
        jax.ShapeDtypeStruct((m, IDX_DIM), F32),
        jax.ShapeDtypeStruct((m, nkv), BF16), jax.ShapeDtypeStruct((m, IDX_DIM), BF16),
        jax.ShapeDtypeStruct((nq, m), BF16), jax.ShapeDtypeStruct((nqi, m), BF16),
        jax.ShapeDtypeStruct((m // kc, N_KV_B, vrows, kc), BF16),
        jax.ShapeDtypeStruct((N_IDX_HEADS, m), F32),
        jax.ShapeDtypeStruct((m // seq_len, 2, D_CONV), F32),
    )
    return pl.pallas_call(
        functools.partial(_even_proj_prompt_body, tiles_per_seq=tiles_per_seq),
        grid=(m // tm,), in_specs=in_specs, out_specs=out_specs, out_shape=out_shape,
        scratch_shapes=[pltpu.VMEM((SUBLANES, D_CONV), F32)],
        compiler_params=_cparams(("arbitrary",)), name="even_proj_prompt",
    )(x, g, w, wt, cw, *tabs, *tabs_t, init)


def _even_proj(x, g, w, wwi, cw, tabs, init, *, seq_len, time_major_b):
    m = x.shape[0]
    tm = m
    nt = 1
    assert seq_len >= 2
    row = lambda i: (i, 0)
    const = lambda i: (0, 0)
    tab_spec = pl.BlockSpec((tm, LANES), const)
    init_spec = pl.BlockSpec((2 * time_major_b, D_CONV), const)
    conv_shape = jax.ShapeDtypeStruct((2 * time_major_b, D_CONV), F32)
    conv_spec = pl.BlockSpec((2 * time_major_b, D_CONV), const)
    scratch = []
    out_shape = (
        jax.ShapeDtypeStruct((m, D_CONV), BF16),
        jax.ShapeDtypeStruct((m, N_HEADS_B * HD_B), BF16),
        jax.ShapeDtypeStruct((m, N_KV_B * HD_B), F32),
        jax.ShapeDtypeStruct((m, N_KV_B * HD_B), F32),
        jax.ShapeDtypeStruct((m, N_IDX_HEADS * IDX_DIM), BF16),
        jax.ShapeDtypeStruct((m, IDX_DIM), F32),
        jax.ShapeDtypeStruct((N_IDX_HEADS, m), F32),
        conv_shape,
    )
    out_specs = (
        pl.BlockSpec((tm, D_CONV), row),
        pl.BlockSpec((tm, N_HEADS_B * HD_B), row),
        pl.BlockSpec((tm, N_KV_B * HD_B), row),
        pl.BlockSpec((tm, N_KV_B * HD_B), row),
        pl.BlockSpec((tm, N_IDX_HEADS * IDX_DIM), row),
        pl.BlockSpec((tm, IDX_DIM), row),
        pl.BlockSpec((N_IDX_HEADS, tm), lambda i: (0, i)),
        conv_spec,
    )
    in_specs = [
        pl.BlockSpec((tm, D_MODEL), row),
        pl.BlockSpec((1, D_MODEL), const),
        pl.BlockSpec((D_MODEL, _EV_END), const),
        pl.BlockSpec((N_IDX_HEADS, D_MODEL), const),
        pl.BlockSpec((CONV_W, D_CONV), const),
        tab_spec, tab_spec, tab_spec,
        init_spec,
    ]
    return pl.pallas_call(
        functools.partial(_even_proj_body, time_major_b=time_major_b),
        grid=(nt,), in_specs=in_specs, out_specs=out_specs, out_shape=out_shape,
        scratch_shapes=scratch, compiler_params=_cparams(("arbitrary",)),
        name="even_proj",
    )(x, g, w, wwi, cw, *tabs, init)


def _mix_out_body(*refs, n_in):
    ins = refs[:n_in]
    ws = refs[n_in:2 * n_in]
    x_ref, g1_ref, g2_ref, wq_ref, x1_ref, qx_ref = refs[2 * n_in:]
    y = _dot(ins[0][...], ws[0][...])
    for a, w in zip(ins[1:], ws[1:]):
        y = y + _dot(a[...], w[...])
    x1 = x_ref[...] + _rms(y, g1_ref[...])
    x1_ref[...] = x1
    xn = _rms(x1, g2_ref[...]).astype(BF16)
    qx_ref[...] = (_dot(xn, wq_ref[...]) * HD_X ** -0.5).astype(BF16)


def _mix_out(ins, ws, x, g1, g2, wq, *, tm):
    m = x.shape[0]
    row = lambda i: (i, 0)
    const = lambda i: (0, 0)
    n_in = len(ins)
    in_specs = ([pl.BlockSpec((tm, a.shape[1]), row) for a in ins]
                + [pl.BlockSpec(w.shape, const) for w in ws]
                + [pl.BlockSpec((tm, D_MODEL), row), pl.BlockSpec((1, D_MODEL), const),
                   pl.BlockSpec((1, D_MODEL), const), pl.BlockSpec(wq.shape, const)])
    nq = wq.shape[1]
    return pl.pallas_call(
        functools.partial(_mix_out_body, n_in=n_in),
        grid=(m // tm,), in_specs=in_specs,
        out_specs=(pl.BlockSpec((tm, D_MODEL), row), pl.BlockSpec((tm, nq), row)),
        out_shape=(jax.ShapeDtypeStruct((m, D_MODEL), F32), jax.ShapeDtypeStruct((m, nq), BF16)),
        compiler_params=_cparams(("parallel",)), name="mix_out",
    )(*ins, *ws, x, g1, g2, wq)


def _mem_kv_body(mem_ref, g_ref, wk_ref, wv_ref, mk_ref, mv_ref):
    mn = _rms(mem_ref[...], g_ref[0]).astype(BF16)
    mk_ref[0] = _dot(mn, wk_ref[0])
    mv_ref[0] = _dot(mn, wv_ref[0])


def _mem_kv(mem, g, wk, wv, *, tm):
    m = mem.shape[0]
    depth = g.shape[0]
    n = wk.shape[2]
    wspec = pl.BlockSpec((1, D_MODEL, n), lambda l, i: (l, 0, 0))
    ospec = pl.BlockSpec((1, tm, n), lambda l, i: (l, i, 0))
    oshape = jax.ShapeDtypeStruct((depth, m, n), F32)
    return pl.pallas_call(
        _mem_kv_body, grid=(depth, m // tm),
        in_specs=[pl.BlockSpec((tm, D_MODEL), lambda l, i: (i, 0)),
                  pl.BlockSpec((1, 1, D_MODEL), lambda l, i: (l, 0, 0)), wspec, wspec],
        out_specs=(ospec, ospec), out_shape=(oshape, oshape),
        compiler_params=_cparams(("parallel", "parallel")), name="mem_kv",
    )(mem, g, wk, wv)


def _xattn_body(q_ref, mk_ref, mv_ref, x_ref, wo_ref, g_ref, o_ref):
    bb, tm = q_ref.shape[0], q_ref.shape[1]
    rows = []
    for j in range(bb):
        q = q_ref[j]
        mk = mk_ref[j].astype(BF16)
        mv = mv_ref[j].astype(BF16)
        outs = []
        for h in range(N_HEADS_X):
            sl = slice(h * HD_X, (h + 1) * HD_X)
            s = _dot_nt(q[:, sl], mk[:, sl])
            s = s - jnp.max(s, axis=-1, keepdims=True)
            p = jnp.exp(s)
            p = p / jnp.sum(p, axis=-1, keepdims=True)
            outs.append(_dot(p.astype(BF16), mv[:, sl]))
        rows.append(jnp.concatenate(outs, axis=-1).astype(BF16))
    o = rows[0] if bb == 1 else jnp.concatenate(rows, axis=0)
    x = x_ref[...].reshape(bb * tm, D_MODEL)
    o_ref[...] = (x + _rms(_dot(o, wo_ref[...]), g_ref[...])).reshape(bb, tm, D_MODEL)


def _xattn(q, mk, mv, x, wo, g, *, tm, bb):
    b, t, _ = x.shape
    nq = q.shape[2]
    tile = lambda i, j: (i, j, 0)
    per_b = lambda i, j: (i, 0, 0)
    const = lambda i, j: (0, 0)
    return pl.pallas_call(
        _xattn_body, grid=(b // bb, t // tm),
        in_specs=[pl.BlockSpec((bb, tm, nq), tile),
                  pl.BlockSpec((bb, N_MEM, nq), per_b), pl.BlockSpec((bb, N_MEM, nq), per_b),
                  pl.BlockSpec((bb, tm, D_MODEL), tile),
                  pl.BlockSpec(wo.shape, const), pl.BlockSpec((1, D_MODEL), const)],
        out_specs=pl.BlockSpec((bb, tm, D_MODEL), tile),
        out_shape=jax.ShapeDtypeStruct((b, t, D_MODEL), F32),
        compiler_params=_cparams(("parallel", "parallel")), name="xattn",
    )(q, mk, mv, x, wo, g)


def _mlp_body(x_ref, g4_ref, g5_ref, w1_ref, w2_ref, o_ref):
    x = x_ref[...]
    xn = _rms(x, g4_ref[...]).astype(BF16)
    h = jnp.maximum(_dot(xn, w1_ref[...]), 0.0)
    y = _dot((h * h).astype(BF16), w2_ref[...])
    o_ref[...] = x + _rms(y, g5_ref[...])


def _mlp(x, g4, g5, w1, w2, *, tm):
    m = x.shape[0]
    row = lambda i: (i, 0)
    const = lambda i: (0, 0)
    resident = lambda a: pl.BlockSpec(a.shape, const, pipeline_mode=pl.Buffered(1))
    return pl.pallas_call(
        _mlp_body, grid=(m // tm,),
        in_specs=[pl.BlockSpec((tm, D_MODEL), row), pl.BlockSpec((1, D_MODEL), const),
                  pl.BlockSpec((1, D_MODEL), const), resident(w1), resident(w2)],
        out_specs=pl.BlockSpec((tm, D_MODEL), row),
        out_shape=jax.ShapeDtypeStruct((m, D_MODEL), F32),
        compiler_params=_cparams(("parallel",)), name="mlp",
    )(x, g4, g5, w1, w2)


def _for_pairs(n, body):
    def pair(j, carry):
        body(2 * j, carry)
        body(2 * j + 1, carry)
        return carry

    lax.fori_loop(0, n // 2, pair, 0)

    @pl.when(n % 2 == 1)
    def _():
        body(n - 1, 0)


def _tree_sum(parts):
    while len(parts) > 1:
        nxt = [a + b for a, b in zip(parts[::2], parts[1::2])]
        if len(parts) % 2:
            nxt.append(parts[-1])
        parts = nxt
    return parts[0]


def _sortable_key(score):
    bits = pltpu.bitcast(score, I32)
    return bits ^ ((bits >> 31) & 0x7FFFFFFF)


def _dsa_prompt_body(qit_ref, wit_ref, ki_ref, qt_ref, k_ref, vt_ref, o_ref,
                     key_ref, hi_ref, qpad_ref, m_ref, acc_ref, knorm_ref, *, ktop, idx_bits):
    tq = qit_ref.shape[1]
    kc = tq
    i = pl.program_id(1)
    nck = i + 1
    kiota = lax.broadcasted_iota(I32, (kc, tq), 0)
    qidx = i * tq + lax.broadcasted_iota(I32, (kc, tq), 1)
    ng = N_HEADS_B // N_KV_B

    def chunk_off(c):
        return pl.multiple_of(c * kc, kc)

    def score_chunk(c, carry):
        off = chunk_off(c)
        kic = ki_ref[pl.ds(off, kc), :]
        sc = jnp.zeros((kc, tq), F32)
        for h in range(N_IDX_HEADS):
            s = _dot(kic, qit_ref[h * IDX_DIM:(h + 1) * IDX_DIM, :])
            sc = sc + wit_ref[h:h + 1, :] * jnp.maximum(s, 0.0)
        sc = jnp.where(kiota + off <= qidx, sc, -jnp.inf)
        bits = pltpu.bitcast(sc, I32)
        key_ref[pl.ds(off, kc), :] = bits ^ ((bits >> 31) & 0x7FFFFFFF)
        hi_ref[pl.ds(off, kc), :] = pltpu.bitcast(bits & _HI16_MASK, F32).astype(BF16)
        return carry

    _for_pairs(nck, score_chunk)

    @pl.when(nck % 2 == 1)
    def _():
        key_ref[pl.ds(chunk_off(nck), kc), :] = jnp.full((kc, tq), INT_MIN, I32)
        hi_ref[pl.ds(chunk_off(nck), kc), :] = jnp.full((kc, tq), jnp.nan, BF16)

    def count(pred):
        def one(c, acc):
            off = chunk_off(c)
            hit = pred(key_ref[pl.ds(off, kc), :], kiota + off).astype(I32)
            return acc + jnp.sum(hit.reshape(kc // SUBLANES, SUBLANES, tq), axis=0)

        def body(j, acc):
            return one(2 * j + 1, one(2 * j, acc))

        acc = lax.fori_loop(0, (nck + 1) // 2, body, jnp.zeros((SUBLANES, tq), I32))
        return jnp.sum(acc, axis=0, keepdims=True)

    rows16 = 2 * SUBLANES
    one16 = jnp.ones((), BF16)
    zero16 = jnp.zeros((), BF16)

    def count_hi(cand_bf16):
        def one(c, acc):
            hit = jnp.where(hi_ref[pl.ds(chunk_off(c), kc), :] >= cand_bf16, one16, zero16)
            return acc + _tree_sum([hit[r * rows16:(r + 1) * rows16] for r in range(kc // rows16)])

        def body(j, acc):
            return one(2 * j + 1, one(2 * j, acc))

        acc = lax.fori_loop(0, (nck + 1) // 2, body, jnp.zeros((rows16, tq), BF16))
        return jnp.sum(acc.astype(F32), axis=0, keepdims=True).astype(I32)

    def hi_step(j, carry):
        lo, cnt_lo = carry
        cand = lo + jnp.left_shift(jnp.int32(1), 15 - j)
        bits16 = (cand ^ ((cand >> 31) & 0x7FFF)) & 0xFFFF
        bits16 = jnp.where(jnp.logical_and(cand >= 1, cand < _MIN_NORMAL16), _MIN_NORMAL16, bits16)
        cand_f = pltpu.bitcast(jnp.left_shift(bits16, 16), F32)
        cnt = count_hi(cand_f.astype(BF16))
        keep = cnt >= ktop
        return jnp.where(keep, cand, lo), jnp.where(keep, cnt, cnt_lo)

    hi16, cnt_hi = lax.fori_loop(
        0, 16, hi_step, (jnp.full((1, tq), -(2 ** 15), I32), jnp.full((1, tq), ktop, I32)))

    def bit_step(j, carry):
        lo, cnt_lo = carry
        cand = lo + jnp.left_shift(jnp.int32(1), 15 - j)
        cnt = count(lambda blk, _: blk >= cand)
        keep = cnt >= ktop
        return jnp.where(keep, cand, lo), jnp.where(keep, cnt, cnt_lo)

    thr, cnt_ge = lax.fori_loop(0, 16, bit_step, (jnp.left_shift(hi16, 16), cnt_hi))

    tie = jnp.logical_and(thr > KEY_NEG_INF, cnt_ge > ktop)

    @pl.when(jnp.max(tie.astype(I32)) > 0)
    def _():
        need = ktop - count(lambda blk, _: blk > thr)

        def idx_step(j, res):
            cand = res + jnp.left_shift(jnp.int32(1), idx_bits - 1 - j)
            cnt = count(lambda blk, kidx: jnp.logical_and(blk == thr, kidx < cand))
            return jnp.where(cnt < need, cand, res)

        jcut = lax.fori_loop(0, idx_bits, idx_step, jnp.zeros((1, tq), I32))

        def drop_chunk(c, carry):
            off = chunk_off(c)
            blk = key_ref[pl.ds(off, kc), :]
            drop = jnp.logical_and(blk == thr, kiota + off > jcut)
            key_ref[pl.ds(off, kc), :] = jnp.where(drop, blk - 1, blk)
            return carry

        lax.fori_loop(0, nck, drop_chunk, 0)

    thr_eff = jnp.maximum(thr, KEY_NEG_INF + 1)

    @pl.when(i == 0)
    def _():
        def knorm_chunk(c, mx):
            kk = k_ref[pl.ds(chunk_off(c), kc), :].astype(F32)
            sq = kk * kk
            per_g = [jnp.max(jnp.sum(sq[:, g * HD_B:(g + 1) * HD_B], axis=1, keepdims=True),
                             axis=0, keepdims=True) for g in range(N_KV_B)]
            return jnp.maximum(mx, jnp.concatenate(per_g, axis=0))

        mx = lax.fori_loop(0, k_ref.shape[0] // kc, knorm_chunk, jnp.zeros((N_KV_B, 1), F32))
        knorm_ref[...] = jnp.broadcast_to(mx, knorm_ref.shape)

    zero_half = jnp.zeros((HD_B, tq), BF16)
    for g in range(N_KV_B):
        cols, bounds = [], []
        for hh in range(ng):
            h = g * ng + hh
            qh = qt_ref[h * HD_B:(h + 1) * HD_B, :]
            halves = [zero_half] * N_KV_B
            halves[g] = qh
            cols.append(jnp.concatenate(halves, axis=0))
            qf = qh.astype(F32)
            qn2 = jnp.sum(qf * qf, axis=0, keepdims=True)
            bounds.append(jnp.sqrt(qn2 * knorm_ref[g:g + 1, 0:1]) * _BOUND_SLACK)
        qpad_ref[g, 0:N_KV_B * HD_B, :] = jnp.concatenate(cols, axis=1)
        neg_bound = jnp.broadcast_to(-jnp.concatenate(bounds, axis=1), (2 * SUBLANES, ng * tq))
        qpad_ref[g, N_KV_B * HD_B:N_KV_B * HD_B + 2 * SUBLANES, :] = jnp.where(
            lax.broadcasted_iota(I32, neg_bound.shape, 0) == 0, neg_bound, 0.0).astype(BF16)
    acc_ref[...] = jnp.zeros(acc_ref.shape, F32)
    ones_cols = jnp.ones((kc, 2 * SUBLANES), BF16)

    def chunk_inputs(c):
        off = chunk_off(c)
        bias = jnp.where(key_ref[pl.ds(off, kc), :] >= thr_eff, 0.0, NEG_BIG)
        return jnp.concatenate([bias] * ng, axis=1), k_ref[pl.ds(off, kc), :]

    def attn_chunk(c, carry):
        bias, kch = chunk_inputs(c)
        kaug = jnp.concatenate([kch, ones_cols], axis=1)
        for g in range(N_KV_B):
            p = jnp.exp2(_dot(kaug, qpad_ref[g]) + bias).astype(BF16)
            acc_ref[g] += _dot(vt_ref[c, g], p)
        return carry

    _for_pairs(nck, attn_chunk)

    @pl.when(jnp.min(acc_ref[:, HD_B:HD_B + 1, :]) < _MIN_MASS)
    def _():
        m_ref[...] = jnp.full(m_ref.shape, NEG_BIG, F32)
        acc_ref[...] = jnp.zeros(acc_ref.shape, F32)

        def exact_chunk(c, carry):
            bias, kch = chunk_inputs(c)
            for g in range(N_KV_B):
                s = _dot(kch, qpad_ref[g, 0:N_KV_B * HD_B, :]) + bias
                m_old = m_ref[g]
                m_new = jnp.maximum(m_old, jnp.max(s, axis=0, keepdims=True))
                alpha = jnp.exp2(m_old - m_new)
                p = jnp.exp2(s - m_new).astype(BF16)
                acc_ref[g] = alpha * acc_ref[g] + _dot(vt_ref[c, g], p)
                m_ref[g] = m_new
            return carry

        lax.fori_loop(0, nck, exact_chunk, 0)

    for g in range(N_KV_B):
        acc = acc_ref[g]
        o = acc[:HD_B] / acc[HD_B:HD_B + 1]
        for hp in range(ng // 2):
            pair = jnp.concatenate([o[:, (2 * hp) * tq:(2 * hp + 1) * tq],
                                    o[:, (2 * hp + 1) * tq:(2 * hp + 2) * tq]], axis=0)
            lane0 = (g * ng + 2 * hp) * HD_B
            o_ref[:, lane0:lane0 + 2 * HD_B] = pair.T.astype(BF16)


def _dsa_prompt(qit, wit, kib, qt, kb, vt, *, b, t, tq):
    hd = qt.shape[0]
    nt = t // tq
    ktop = min(TOPK_MAX, t // 4)
    idx_bits = max(1, (t - 1).bit_length())
    assert t // (2 * SUBLANES) <= 256, "bf16 hit counters are exact only up to 256"
    ng = N_HEADS_B // N_KV_B
    vrows = vt.shape[2]
    col = lambda bi, i: (0, bi * nt + i)
    return pl.pallas_call(
        functools.partial(_dsa_prompt_body, ktop=ktop, idx_bits=idx_bits),
        grid=(b, nt),
        in_specs=[
            pl.BlockSpec((hd, tq), col),
            pl.BlockSpec((N_IDX_HEADS, tq), col),
            pl.BlockSpec((t, IDX_DIM), lambda bi, i: (bi, 0)),
            pl.BlockSpec((hd, tq), col),
            pl.BlockSpec((t, N_KV_B * HD_B), lambda bi, i: (bi, 0)),
            pl.BlockSpec((nt, N_KV_B, vrows, tq), lambda bi, i: (bi, 0, 0, 0)),
        ],
        out_specs=pl.BlockSpec((tq, hd), lambda bi, i: (bi * nt + i, 0)),
        out_shape=jax.ShapeDtypeStruct((b * t, hd), BF16),
        scratch_shapes=[pltpu.VMEM((t + tq, tq), I32),
                        pltpu.VMEM((t + tq, tq), BF16),
                        pltpu.VMEM((N_KV_B, N_KV_B * HD_B + 2 * SUBLANES, ng * tq), BF16),
                        pltpu.VMEM((N_KV_B, 1, ng * tq), F32),
                        pltpu.VMEM((N_KV_B, vrows, ng * tq), F32),
                        pltpu.VMEM((N_KV_B, LANES), F32)],
        compiler_params=_cparams(("parallel", "arbitrary")), name="dsa_prompt",
    )(qit, wit, kib, qt, kb, vt)


_TP = SUBLANES
_NB = LANES
_PG_MAX = 16


def _dsa_sample_body(pt_ref, qi_ref, w_ref, kin_ref, q_ref, kn_ref, vn_ref,
                     ci_hbm, ck_hbm, cv_hbm, o_ref,
                     ibuf, kbuf, vbuf, key_ref, keyn_ref, isem, ksem, vsem,
                     *, n_pages, pg, t_new, ktop, idx_bits):
    b = pl.program_id(0)
    slot = b % 2
    ck = pg * PAGE_SIZE
    nch = n_pages // pg
    past = n_pages * PAGE_SIZE

    def idx_copy(sl, p, page):
        return pltpu.make_async_copy(ci_hbm.at[page], ibuf.at[sl, p], isem.at[sl])

    def k_copy(p, page):
        return pltpu.make_async_copy(ck_hbm.at[page], kbuf.at[p], ksem)

    def v_copy(p, page):
        return pltpu.make_async_copy(cv_hbm.at[page], vbuf.at[p], vsem)

    def issue_idx(bb, sl):
        def f(p, carry):
            idx_copy(sl, p, pt_ref[bb, p]).start()
            return carry

        lax.fori_loop(0, n_pages, f, 0)

    @pl.when(b == 0)
    def _():
        issue_idx(0, 0)

    def issue_kv(p, carry):
        page = pt_ref[b, p]
        k_copy(p, page).start()
        v_copy(p, page).start()
        return carry

    lax.fori_loop(0, n_pages, issue_kv, 0)

    @pl.when(b + 1 < pl.num_programs(0))
    def _():
        issue_idx(b + 1, 1 - slot)

    def wait_pages(hbm, buf, sem):
        pltpu.make_async_copy(hbm.at[pl.ds(0, n_pages)], buf, sem).wait()

    lane = lax.broadcasted_iota(I32, (_TP, ck), 1)
    lane_n = lax.broadcasted_iota(I32, (_TP, _NB), 1)
    row_n = lax.broadcasted_iota(I32, (_TP, _NB), 0)

    qi = qi_ref[0]
    nt = qi.shape[0] // N_IDX_HEADS

    def scores(kpt):
        n = kpt.shape[1]
        s = jnp.maximum(_dot(qi, kpt), 0.0) * w_ref[0]
        return jnp.sum(s.reshape(N_IDX_HEADS, nt, n), axis=0)

    def pages_t(buf, first):
        return jnp.concatenate([buf[first + p] for p in range(pg)], axis=1).astype(BF16)

    wait_pages(ci_hbm, ibuf.at[slot], isem.at[slot])

    real_row = lax.broadcasted_iota(I32, (_TP, ck), 0) < t_new

    def sc_chunk(c, carry):
        sc = scores(pages_t(ibuf.at[slot], c * pg))
        key_ref[c] = _sortable_key(jnp.where(real_row, sc, -jnp.inf))
        return carry

    lax.fori_loop(0, nch, sc_chunk, 0)
    admissible = jnp.logical_and(lane_n <= row_n, row_n < t_new)
    keyn_ref[...] = _sortable_key(jnp.where(admissible, scores(kin_ref[0]), -jnp.inf))

    def count(pred):
        acc = jnp.zeros((_TP, ck), I32)
        for c in range(nch):
            acc = acc + pred(key_ref[c], lane + c * ck).astype(I32)
        acc_n = pred(keyn_ref[...], lane_n + past).astype(I32)
        return jnp.sum(acc, axis=1, keepdims=True) + jnp.sum(acc_n, axis=1, keepdims=True)

    def bit_step(j, carry):
        lo, cnt_lo = carry
        cand = lo + jnp.left_shift(jnp.int32(1), 31 - j)
        cnt = count(lambda blk, _: blk >= cand)
        keep = cnt >= ktop
        return jnp.where(keep, cand, lo), jnp.where(keep, cnt, cnt_lo)

    thr, cnt_ge = lax.fori_loop(
        0, 32, bit_step, (jnp.full((_TP, 1), INT_MIN, I32), jnp.full((_TP, 1), ktop, I32)))
    tie = jnp.logical_and(thr > KEY_NEG_INF, cnt_ge > ktop)

    @pl.when(jnp.max(tie.astype(I32)) > 0)
    def _():
        need = ktop - count(lambda blk, _: blk > thr)

        def idx_step(j, res):
            cand = res + jnp.left_shift(jnp.int32(1), idx_bits - 1 - j)
            cnt = count(lambda blk, kidx: jnp.logical_and(blk == thr, kidx < cand))
            return jnp.where(cnt < need, cand, res)

        jcut = lax.fori_loop(0, idx_bits, idx_step, jnp.zeros((_TP, 1), I32))

        def dropped(blk, kidx):
            return jnp.where(jnp.logical_and(blk == thr, kidx > jcut), blk - 1, blk)

        def drop_chunk(c, carry):
            key_ref[c] = dropped(key_ref[c], lane + c * ck)
            return carry

        lax.fori_loop(0, nch, drop_chunk, 0)
        keyn_ref[...] = dropped(keyn_ref[...], lane_n + past)

    thr_eff = jnp.maximum(thr, KEY_NEG_INF + 1)
    qp = q_ref[0]
    reps = qp.shape[0] // _TP

    def attend(keys, kct, vct, carry):
        m, l, acc = carry
        bias = jnp.where(keys >= thr_eff, 0.0, NEG_BIG)
        s = _dot(qp, kct) + jnp.concatenate([bias] * reps, axis=0)
        m_new = jnp.maximum(m, jnp.max(s, axis=-1, keepdims=True))
        alpha = jnp.exp2(m - m_new)
        p = jnp.exp2(s - m_new)
        l = alpha * l + jnp.sum(p, axis=-1, keepdims=True)
        acc = alpha * acc + _dot_nt(p.astype(BF16), vct)
        return m_new, l, acc

    wait_pages(ck_hbm, kbuf, ksem)
    wait_pages(cv_hbm, vbuf, vsem)

    def at_chunk(c, carry):
        return attend(key_ref[c], pages_t(kbuf, c * pg), pages_t(vbuf, c * pg), carry)

    nr = qp.shape[0]
    init = (jnp.full((nr, 1), NEG_BIG, F32), jnp.zeros((nr, 1), F32),
            jnp.zeros((nr, N_KV_B * HD_B), F32))
    carry = lax.fori_loop(0, nch, at_chunk, init)
    m, l, acc = attend(keyn_ref[...], kn_ref[0], vn_ref[0], carry)
    o_ref[0] = acc / l


def _dsa_sample(page_table, qi, w, kin, qp, kn, vn, cache_i, cache_k, cache_v, *, t_new):
    b, n_pages = page_table.shape
    pg = math.gcd(n_pages, _PG_MAX)
    nr = qp.shape[1]
    total = n_pages * PAGE_SIZE + t_new
    ktop = min(TOPK_MAX, total // 4)
    idx_bits = max(1, (total - 1).bit_length())
    nch = n_pages // pg
    per_b = lambda i, pt: (i, 0, 0)
    grid_spec = pltpu.PrefetchScalarGridSpec(
        num_scalar_prefetch=1, grid=(b,),
        in_specs=[pl.BlockSpec((1,) + qi.shape[1:], per_b), pl.BlockSpec((1,) + w.shape[1:], per_b),
                  pl.BlockSpec((1,) + kin.shape[1:], per_b), pl.BlockSpec((1,) + qp.shape[1:], per_b),
                  pl.BlockSpec((1,) + kn.shape[1:], per_b), pl.BlockSpec((1,) + vn.shape[1:], per_b),
                  pl.BlockSpec(memory_space=pl.ANY), pl.BlockSpec(memory_space=pl.ANY),
                  pl.BlockSpec(memory_space=pl.ANY)],
        out_specs=pl.BlockSpec((1, nr, N_KV_B * HD_B), per_b),
        scratch_shapes=[pltpu.VMEM((2, n_pages, IDX_DIM, PAGE_SIZE), F32),
                        pltpu.VMEM((n_pages, N_KV_B * HD_B, PAGE_SIZE), F32),
                        pltpu.VMEM((n_pages, N_KV_B * HD_B, PAGE_SIZE), F32),
                        pltpu.VMEM((nch, _TP, pg * PAGE_SIZE), I32),
                        pltpu.VMEM((_TP, _NB), I32),
                        pltpu.SemaphoreType.DMA((2,)),
                        pltpu.SemaphoreType.DMA(()),
                        pltpu.SemaphoreType.DMA(())])
    return pl.pallas_call(
        functools.partial(_dsa_sample_body, n_pages=n_pages, pg=pg, t_new=t_new, ktop=ktop,
                          idx_bits=idx_bits),
        grid_spec=grid_spec,
        out_shape=jax.ShapeDtypeStruct((b, nr, N_KV_B * HD_B), F32),
        compiler_params=_cparams(("arbitrary",)), name="dsa_sample",
    )(page_table, qi, w, kin, qp, kn, vn, cache_i, cache_k, cache_v)


def _log_sigmoid(x):
    return jnp.minimum(x, 0.0) - jnp.log1p(jnp.exp(-jnp.abs(x)))


def _odd_proj_body(x_ref, g_ref, w_ref, wt_ref, wg_ref, bg_ref, bgt_ref,
                   qt_ref, k_ref, vt_ref, so_ref, gate_ref, gatet_ref):
    xn = _rms(x_ref[...], g_ref[...]).astype(BF16)
    nqk = N_HEADS_C * DQK_C
    k_ref[...] = _dot(xn, w_ref[:, :nqk]).astype(BF16)
    so_ref[...] = jax.nn.sigmoid(_dot(xn, w_ref[:, nqk:])).astype(BF16)
    nv = N_HEADS_C * DV_C
    allt = _dot_nt(wt_ref[...], xn)
    qt_ref[...] = (allt[:nqk] * DQK_C ** -0.5).astype(BF16)
    vt_ref[...] = allt[nqk:nqk + nv].astype(BF16)
    nh = N_HEADS_C
    gate = _dot(xn, wg_ref[...]) + bg_ref[...]
    col = lax.broadcasted_iota(I32, gate.shape, 1)
    gate_ref[...] = jnp.where(col < nh, gate, _log_sigmoid(gate))
    gatet = allt[nqk + nv:] + bgt_ref[...]
    rowi = lax.broadcasted_iota(I32, gatet.shape, 0)
    gatet_ref[...] = jnp.where(rowi < nh, gatet, _log_sigmoid(gatet))


def _odd_proj(x, g, w, wt, wg, bg, bgt, *, tm):
    m = x.shape[0]
    row = lambda i: (i, 0)
    col = lambda i: (0, i)
    const = lambda i: (0, 0)
    nqk = N_HEADS_C * DQK_C
    nv = N_HEADS_C * DV_C
    ng = 2 * N_HEADS_C
    return pl.pallas_call(
        _odd_proj_body, grid=(m // tm,),
        in_specs=[pl.BlockSpec((tm, D_MODEL), row), pl.BlockSpec((1, D_MODEL), const),
                  pl.BlockSpec(w.shape, const), pl.BlockSpec(wt.shape, const),
                  pl.BlockSpec((D_MODEL, ng), const), pl.BlockSpec((1, ng), const),
                  pl.BlockSpec((ng, 1), const)],
        out_specs=(pl.BlockSpec((nqk, tm), col), pl.BlockSpec((tm, nqk), row),
                   pl.BlockSpec((nv, tm), col), pl.BlockSpec((tm, nv), row),
                   pl.BlockSpec((tm, ng), row), pl.BlockSpec((ng, tm), col)),
        out_shape=(jax.ShapeDtypeStruct((nqk, m), BF16), jax.ShapeDtypeStruct((m, nqk), BF16),
                   jax.ShapeDtypeStruct((nv, m), BF16), jax.ShapeDtypeStruct((m, nv), BF16),
                   jax.ShapeDtypeStruct((m, ng), F32), jax.ShapeDtypeStruct((ng, m), F32)),
        compiler_params=_cparams(("parallel",)), name="odd_proj",
    )(x, g, w, wt, wg, bg, bgt)


def _split3(x):
    hi = x.astype(BF16)
    r1 = x - hi.astype(F32)
    mid = r1.astype(BF16)
    lo = (r1 - mid.astype(F32)).astype(BF16)
    return hi, mid, lo


def _mlstm_body(qt_ref, k_ref, vt_ref, so_ref, gate_ref, gatet_ref, hg_ref,
                c0_ref, n0_ref, m0_ref,
                h_ref, c_ref, n_ref, m_ref, cs_ref, ns_ref, ms_ref):
    lc = k_ref.shape[0]
    nh = N_HEADS_C
    c = pl.program_id(1)

    @pl.when(c == 0)
    def _():
        cs_ref[...] = c0_ref[0]
        ns_ref[...] = n0_ref[0]
        ms_ref[...] = jnp.broadcast_to(m0_ref[0], ms_ref.shape)

    ri = lax.broadcasted_iota(I32, (lc, lc), 0)
    ci = lax.broadcasted_iota(I32, (lc, lc), 1)
    causal_t = ri <= ci
    tril = (ci <= ri).astype(BF16)
    triu = causal_t.astype(BF16)
    gate = gate_ref[...]
    gatet = gatet_ref[...]
    lf3 = _split3(gate[:, nh:])
    bcols = _dot(tril, lf3[0]) + _dot(tril, lf3[1]) + _dot(tril, lf3[2])
    lft3 = _split3(gatet[nh:, :])
    brows = _dot(lft3[0], triu) + _dot(lft3[1], triu) + _dot(lft3[2], triu)
    half_lane = lax.broadcasted_iota(I32, (1, 2 * DQK_C), 1) // DQK_C
    zero_q = jnp.zeros((DQK_C, lc), BF16)
    ms = ms_ref[...]
    ms_new = []
    for h in range(nh):
        j, half = divmod(h, 2)
        in_half = half_lane == half
        kp = k_ref[:, j * 2 * DQK_C:(j + 1) * 2 * DQK_C]
        qh = qt_ref[h * DQK_C:(h + 1) * DQK_C, :]
        qpad = jnp.concatenate([qh, zero_q] if half == 0 else [zero_q, qh], axis=0)
        vt = vt_ref[h * DV_C:(h + 1) * DV_C, :]
        br = brows[h:h + 1, :]
        igr = gatet[h:h + 1, :]
        m_prev = ms[h:h + 1, 0:1]
        a = br + m_prev
        src = gate[:, h:h + 1] - bcols[:, h:h + 1]
        d = jnp.where(causal_t, br + src, NEG_BIG)
        mj = jnp.maximum(a, jnp.max(d, axis=0, keepdims=True))
        s = _dot(kp, qpad) * jnp.exp(d - mj)
        aw = jnp.exp(a - mj)
        cp = cs_ref[j]
        n8 = jnp.broadcast_to(ns_ref[j:j + 1, :], (SUBLANES, 2 * DQK_C)).astype(BF16)
        num = _dot(vt, s.astype(BF16)) + aw * _dot(cp.astype(BF16), qpad)
        den = jnp.sum(s, axis=0, keepdims=True) + aw * _dot(n8, qpad)[0:1]
        ht = num / jnp.maximum(jnp.abs(den), jnp.exp(-mj))
        ht = ht * lax.rsqrt(jnp.mean(ht * ht, axis=0, keepdims=True) + NORM_EPS)
        sl = slice(h * DV_C, (h + 1) * DV_C)
        h_ref[:, sl] = (ht.T * hg_ref[:, sl] * so_ref[:, sl].astype(F32)).astype(BF16)
        b_last = br[:, lc - 1:lc]
        g_row = b_last - br + igr
        m_new = jnp.maximum(b_last + m_prev, jnp.max(g_row, axis=-1, keepdims=True))
        gw = jnp.exp(g_row - m_new)
        decay = jnp.exp(b_last + m_prev - m_new)
        upd = _dot((vt.astype(F32) * gw).astype(BF16), kp)
        cs_ref[j] = jnp.where(in_half, decay * cp + upd, cp)
        gw8 = jnp.broadcast_to(gw, (SUBLANES, lc)).astype(BF16)
        n_old = ns_ref[j:j + 1, :]
        ns_ref[j:j + 1, :] = jnp.where(in_half, decay * n_old + _dot(gw8, kp)[0:1], n_old)
        ms_new.append(jnp.broadcast_to(m_new, (1, ms.shape[1])))
    ms_ref[...] = jnp.concatenate(ms_new, axis=0)

    @pl.when(c == pl.num_programs(1) - 1)
    def _():
        c_ref[0] = cs_ref[...]
        n_ref[0] = ns_ref[...]
        m_ref[0] = ms_ref[...]


def _mlstm(qt, k, vt, so, gate, gatet, hg, c0p, n0p, m0, *, b, t, lc):
    nh = N_HEADS_C
    nqk = nh * DQK_C
    nv = nh * DV_C
    ng = gate.shape[1]
    nc = t // lc
    npair = nh // 2
    row = lambda bi, c: (bi * nc + c, 0)
    col = lambda bi, c: (0, bi * nc + c)
    per_b4 = lambda bi, c: (bi, 0, 0, 0)
    per_b3 = lambda bi, c: (bi, 0, 0)
    return pl.pallas_call(
        _mlstm_body, grid=(b, nc),
        in_specs=[pl.BlockSpec((nqk, lc), col), pl.BlockSpec((lc, nqk), row),
                  pl.BlockSpec((nv, lc), col), pl.BlockSpec((lc, nv), row),
                  pl.BlockSpec((lc, ng), row), pl.BlockSpec((ng, lc), col),
                  pl.BlockSpec((1, nv), lambda bi, c: (0, 0)),
                  pl.BlockSpec((1, npair, DV_C, 2 * DQK_C), per_b4),
                  pl.BlockSpec((1, npair, 2 * DQK_C), per_b3),
                  pl.BlockSpec((1, nh, 1), per_b3)],
        out_specs=(pl.BlockSpec((lc, nv), row),
                   pl.BlockSpec((1, npair, DV_C, 2 * DQK_C), per_b4),
                   pl.BlockSpec((1, npair, 2 * DQK_C), per_b3),
                   pl.BlockSpec((1, nh, LANES), per_b3)),
        out_shape=(jax.ShapeDtypeStruct((b * t, nv), BF16),
                   jax.ShapeDtypeStruct((b, npair, DV_C, 2 * DQK_C), F32),
                   jax.ShapeDtypeStruct((b, npair, 2 * DQK_C), F32),
                   jax.ShapeDtypeStruct((b, nh, LANES), F32)),
        scratch_shapes=[pltpu.VMEM((npair, DV_C, 2 * DQK_C), F32),
                        pltpu.VMEM((npair, 2 * DQK_C), F32), pltpu.VMEM((nh, LANES), F32)],
        compiler_params=_cparams(("parallel", "arbitrary")), name="mlstm",
    )(qt, k, vt, so, gate, gatet, hg, c0p, n0p, m0)


def _prep_even(w_in, w_out):
    sizes = [D_CONV, D_CONV, D_CONV, N_HEADS_B * HD_B, N_KV_B * HD_B, N_KV_B * HD_B,
             N_IDX_HEADS * IDX_DIM, N_IDX_HEADS]
    offs = [sum(sizes[:j + 1]) for j in range(len(sizes))]
    u, gb, gc, q, k, v, qi, wi, ki = jnp.split(w_in, offs, axis=1)
    pad = jnp.zeros((D_MODEL, LANES - IDX_DIM), w_in.dtype)
    w = jnp.concatenate([u, gb, gc, q, k, v, qi, ki, pad], axis=1).astype(BF16)
    wpad = jnp.zeros((D_MODEL, 2 * SUBLANES - N_IDX_HEADS), w_in.dtype)
    wt = jnp.concatenate([q, qi, v, wi, wpad], axis=1).T.astype(BF16)
    return (w, wt, wi.T.astype(BF16), w_out[:D_CONV].astype(BF16), w_out[D_CONV:].astype(BF16))


def _prep_odd(w_in, b_i, b_f):
    q, k, v, o, wg = jnp.split(w_in, [_OD_K, _OD_V, _OD_O, _OD_END], axis=1)
    w = jnp.concatenate([k, o], axis=1).astype(BF16)
    wt = jnp.concatenate([q, v, wg], axis=1).T.astype(BF16)
    bg = jnp.concatenate([b_i, b_f])
    return w, wt, wg.astype(BF16), bg[None, :], bg[:, None]


def _tile(n, pref):
    return pref if n % pref == 0 else n


def _pad_axis(a, axis, n):
    if a.shape[axis] == n:
        return a
    widths = [(0, 0)] * a.ndim
    widths[axis] = (0, n - a.shape[axis])
    return jnp.pad(a, widths)


def _tail(x2, qx, mk, mv, wo, g3, g4, g5, w1, w2, *, b, t):
    t_pad = max(t, 2 * SUBLANES)
    tx = _tile(t_pad, 512)
    q3 = _pad_axis(qx.reshape(b, t, -1), 1, t_pad)
    x3 = _pad_axis(x2.reshape(b, t, D_MODEL), 1, t_pad)
    bb = 1 if tx >= LANES else math.gcd(b, LANES // tx)
    x3 = _xattn(q3, mk, mv, x3, wo, g3, tm=tx, bb=bb)[:, :t].reshape(b * t, D_MODEL)
    return _mlp(x3, g4, g5, w1, w2, tm=_tile(b * t, 512))


def _even_prompt(x2, g, we, cw, wxq, *, b, t, tm):
    w, wt, wwi, wo_a, wo_b = we
    tq = _tile(t, 256)
    tabs = _rope_tables(jnp.arange(t))
    tabs_t = tuple(tab.T for tab in tabs)
    init = jnp.zeros((b, CONV_W - 1, D_CONV), F32)
    ya, k, v, ki, kb, kib, qt, qit, vt, wit, conv = _even_proj_prompt(
        x2, g[0:1], w, wt, cw, tabs, tabs_t, init, tm=tm, seq_len=t, kc=tq)
    yb = _dsa_prompt(qit, wit, kib, qt, kb, vt, b=b, t=t, tq=tq)
    x1, qx = _mix_out([ya, yb], [wo_a, wo_b], x2, g[1:2], g[2:3], wxq, tm=tm)
    state = (k.reshape(b, t, N_KV_B, HD_B), v.reshape(b, t, N_KV_B, HD_B),
             ki.reshape(b, t, IDX_DIM), conv)
    return x1, qx, state


def _odd_mixer(x2, g, wod, hg, w_out, wxq, c0, n0, m0, *, b, t, tm, lc):
    w, wt, wg, bg, bgt = wod
    qt, k, vt, so, gate, gatet = _odd_proj(x2, g[0:1], w, wt, wg, bg, bgt, tm=tm)
    nh = N_HEADS_C
    tp = -(-t // lc) * lc
    if tp > t:
        rows = lambda a: _pad_axis(a.reshape(b, t, -1), 1, tp).reshape(b * tp, -1)
        cols = lambda a: _pad_axis(a.reshape(-1, b, t), 2, tp).reshape(-1, b * tp)
        neutral = jnp.concatenate([jnp.full((nh,), NEG_BIG, F32), jnp.zeros((nh,), F32)])
        gate = jnp.concatenate(
            [gate.reshape(b, t, 2 * nh),
             jnp.broadcast_to(neutral[None, None, :], (b, tp - t, 2 * nh))], axis=1)
        gatet = jnp.concatenate(
            [gatet.reshape(2 * nh, b, t),
             jnp.broadcast_to(neutral[:, None, None], (2 * nh, b, tp - t))], axis=2)
        qt, k, vt, so = cols(qt), rows(k), cols(vt), rows(so)
        gate, gatet = gate.reshape(b * tp, 2 * nh), gatet.reshape(2 * nh, b * tp)
    c0p = c0.reshape(b, nh // 2, 2, DV_C, DQK_C).transpose(0, 1, 3, 2, 4)
    c0p = c0p.reshape(b, nh // 2, DV_C, 2 * DQK_C)
    h, cp, n, m = _mlstm(qt, k, vt, so, gate, gatet, hg, c0p, n0.reshape(b, nh // 2, 2 * DQK_C),
                         m0[:, :, None], b=b, t=tp, lc=lc)
    h2 = h.reshape(b, tp, nh * DV_C)[:, :t].reshape(b * t, nh * DV_C)
    x1, qx = _mix_out([h2], [w_out], x2, g[1:2], g[2:3], wxq, tm=tm)
    c_out = cp.reshape(b, nh // 2, DV_C, 2, DQK_C).transpose(0, 1, 3, 2, 4)
    c_out = c_out.reshape(b, nh, DV_C, DQK_C)
    return x1, qx, (c_out, n.reshape(b, nh, DQK_C), m[:, :, 0])


def _even_sample(x2, g, we, cw, wxq, cache_i, cache_k, cache_v, state_conv, page_table, *, b, t):
    w, _, wwi, wo_a, wo_b = we
    n_pages = page_table.shape[1]
    pos = n_pages * PAGE_SIZE + jnp.arange(t)
    tabs = tuple(jnp.repeat(tab, b, axis=0) for tab in _rope_tables(pos))
    to_tm = lambda a: a.reshape(b, t, -1).transpose(1, 0, 2).reshape(t * b, -1)
    to_bm = lambda a: a.reshape(t, b, -1).transpose(1, 0, 2)
    init = state_conv.transpose(1, 0, 2).reshape(2 * b, D_CONV)
    ya, q, k, v, qi, ki, wit, conv = _even_proj(
        to_tm(x2), g[0:1], w, wwi, cw, tabs, init, seq_len=t, time_major_b=b)
    ng = N_HEADS_B // N_KV_B
    qi_b = to_bm(qi).reshape(b, t, N_IDX_HEADS, IDX_DIM).transpose(0, 2, 1, 3)
    qi_b = _pad_axis(qi_b, 2, _TP).reshape(b, N_IDX_HEADS * _TP, IDX_DIM)
    w_b = _pad_axis(wit.reshape(N_IDX_HEADS, t, b).transpose(2, 0, 1), 2, _TP)
    w_b = w_b.reshape(b, N_IDX_HEADS * _TP, 1)
    new_t = lambda a: _pad_axis(to_bm(a), 1, _NB).transpose(0, 2, 1).astype(BF16)
    kin, kn, vn = new_t(ki), new_t(k), new_t(v)
    q5 = _pad_axis(to_bm(q).reshape(b, t, N_KV_B, ng, HD_B).transpose(0, 2, 3, 1, 4), 3, _TP)
    zq = jnp.zeros_like(q5[:, 0])
    qp = jnp.stack([jnp.concatenate([q5[:, 0], zq], axis=-1),
                    jnp.concatenate([zq, q5[:, 1]], axis=-1)], axis=1)
    qp = qp.reshape(b, N_KV_B * ng * _TP, N_KV_B * HD_B)
    n_phys = cache_k.shape[0]
    pool_t = lambda a: a.reshape(n_phys, PAGE_SIZE, -1).transpose(0, 2, 1)
    o = _dsa_sample(page_table, qi_b, w_b, kin, qp, kn, vn,
                    pool_t(cache_i), pool_t(cache_k), pool_t(cache_v), t_new=t)
    o = o.reshape(b, N_KV_B, ng, _TP, N_KV_B, HD_B)[:, :, :, :t]
    yb = jnp.stack([o[:, 0, :, :, 0], o[:, 1, :, :, 1]], axis=1)
    yb = yb.transpose(3, 0, 1, 2, 4).reshape(t * b, N_HEADS_B * HD_B).astype(BF16)
    x1, qx = _mix_out([ya, yb], [wo_a, wo_b], to_tm(x2), g[1:2], g[2:3], wxq, tm=t * b)
    back = lambda a: to_bm(a).reshape(b * t, -1)
    state = (to_bm(k).reshape(b, t, N_KV_B, HD_B), to_bm(v).reshape(b, t, N_KV_B, HD_B),
             to_bm(ki), conv.reshape(2, b, D_CONV).transpose(1, 0, 2))
    return back(x1), back(qx), state


def kernel(x_prompt, x_sample, cache_k, cache_v, cache_idx_k, cache_mem_k, cache_mem_v, state_conv, state_C, state_n, state_m, page_table, mem_prompt, norm_g, w_in_even, conv_w, w_out_even, w_in_odd, b_i, b_f, hnorm_g, w_out_odd, mem_norm_g, w_xq, w_xk, w_xv, w_xo, w_ff1, w_ff2):
    bp, tp, _ = x_prompt.shape
    bs, ts, _ = x_sample.shape
    depth = norm_g.shape[0]
    tm_p = _tile(tp, 512)
    lc = _tile(tp, 256)
    nmem = mem_prompt.shape[1]
    mk_all, mv_all = _mem_kv(mem_prompt.reshape(bp * nmem, D_MODEL), mem_norm_g[:, None, :],
                             w_xk.astype(BF16), w_xv.astype(BF16), tm=_tile(bp * nmem, 512))
    xp = x_prompt.reshape(bp * tp, D_MODEL)
    xs = x_sample.reshape(bs * ts, D_MODEL)
    ev_p, ev_s, od_p, od_s = [], [], [], []
    for l in range(depth):
        g = norm_g[l]
        wxq = w_xq[l].astype(BF16)
        if l % 2 == 0:
            e = l // 2
            we = _prep_even(w_in_even[e], w_out_even[e])
            xp, qxp, st = _even_prompt(xp, g, we, conv_w[e], wxq, b=bp, t=tp, tm=tm_p)
            ev_p.append(st)
            xs, qxs, st = _even_sample(xs, g, we, conv_w[e], wxq, cache_idx_k[e], cache_k[e],
                                       cache_v[e], state_conv[e], page_table, b=bs, t=ts)
            ev_s.append(st)
        else:
            o = l // 2
            wod = _prep_odd(w_in_odd[o], b_i[o], b_f[o])
            hg = hnorm_g[o][None, :]
            wout = w_out_odd[o].astype(BF16)
            zc = jnp.zeros((bp, N_HEADS_C, DV_C, DQK_C), F32)
            zn = jnp.zeros((bp, N_HEADS_C, DQK_C), F32)
            zm = jnp.zeros((bp, N_HEADS_C), F32)
            xp, qxp, st = _odd_mixer(xp, g, wod, hg, wout, wxq, zc, zn, zm, b=bp, t=tp, tm=tm_p, lc=lc)
            od_p.append(st)
            xs, qxs, st = _odd_mixer(xs, g, wod, hg, wout, wxq, state_C[o], state_n[o], state_m[o],
                                     b=bs, t=ts, tm=bs * ts, lc=LANES)
            od_s.append(st)
        wxo = w_xo[l].astype(BF16)
        w1 = w_ff1[l].astype(BF16)
        w2 = w_ff2[l].astype(BF16)
        nx = N_HEADS_X * HD_X
        xp = _tail(xp, qxp, mk_all[l].reshape(bp, nmem, nx), mv_all[l].reshape(bp, nmem, nx),
                   wxo, g[3:4], g[4:5], g[5:6], w1, w2, b=bp, t=tp)
        xs = _tail(xs, qxs, cache_mem_k[l].reshape(bs, nmem, nx), cache_mem_v[l].reshape(bs, nmem, nx),
                   wxo, g[3:4], g[4:5], g[5:6], w1, w2, b=bs, t=ts)
    stack = lambda sts, j: jnp.stack([s[j] for s in sts])
    mem_shape = (depth, bp, nmem, N_HEADS_X, HD_X)
    return (xp.reshape(bp, tp, D_MODEL), xs.reshape(bs, ts, D_MODEL),
            stack(ev_p, 0), stack(ev_p, 1), stack(ev_p, 2), stack(ev_p, 3),
            stack(od_p, 0), stack(od_p, 1), stack(od_p, 2),
            mk_all.reshape(mem_shape), mv_all.reshape(mem_shape),
            stack(ev_s, 0), stack(ev_s, 1), stack(ev_s, 2), stack(ev_s, 3),
            stack(od_s, 0), stack(od_s, 1), stack(od_s, 2))
```

```python
import functools
import math

import jax
import jax.numpy as jnp
from jax import lax
from jax.experimental import pallas as pl
from jax.experimental.pallas import tpu as pltpu

F32 = jnp.float32
BF16 = jnp.bfloat16
I32 = jnp.int32

D_MODEL = 1024
D_CONV = 512
CONV_W = 3
N_HEADS_B = 8
N_KV_B = 2
HD_B = 64
N_IDX_HEADS = 8
IDX_DIM = 64
TOPK_MAX = 256
N_HEADS_C = 8
DQK_C = 64
DV_C = 128
N_MEM = 256
N_HEADS_X = 4
HD_X = 128
D_FF = 4096
PAGE_SIZE = 128
ROPE_THETA = 500000.0
NORM_EPS = 1e-6

LANES = 128
SUBLANES = 8
VMEM_LIMIT = 48 * 1024 * 1024

NEG_BIG = -1e30
INT_MIN = -(2 ** 31)
KEY_NEG_INF = -(2 ** 31) + 0x007FFFFF

_EV_U, _EV_GB, _EV_GC, _EV_Q, _EV_K, _EV_V, _EV_QI, _EV_KI, _EV_END = (
    0, 512, 1024, 1536, 2048, 2176, 2304, 2816, 2944)
_OD_Q, _OD_K, _OD_V, _OD_O, _OD_END = 0, 512, 1024, 2048, 3072


def _cparams(sem, vmem=VMEM_LIMIT):
    return pltpu.CompilerParams(dimension_semantics=sem, vmem_limit_bytes=vmem)


def _rms(x, g):
    return x * lax.rsqrt(jnp.mean(x * x, axis=-1, keepdims=True) + NORM_EPS) * g


def _dot(a, b):
    return jnp.dot(a, b, preferred_element_type=F32)


def _dot_nt(a, b):
    return lax.dot_general(a, b, (((1,), (1,)), ((), ())), preferred_element_type=F32)


def _rope128(x, c, sa, sb):
    return x * c + pltpu.roll(x, LANES - 8, 1) * sa + pltpu.roll(x, 8, 1) * sb


def _rope_tables(pos):
    r = HD_B // 4
    half = r // 2
    freqs = ROPE_THETA ** (-jnp.arange(half, dtype=F32) * 2.0 / r)
    ang = pos.astype(F32)[:, None] * freqs[None, :]
    cos, sin = jnp.cos(ang), jnp.sin(ang)
    t = pos.shape[0]
    ones = jnp.ones((t, HD_B - r), F32)
    zeros = jnp.zeros((t, HD_B - r), F32)
    zh = jnp.zeros((t, half), F32)
    c = jnp.concatenate([cos, cos, ones], axis=1)
    sa = jnp.concatenate([-sin, zh, zeros], axis=1)
    sb = jnp.concatenate([zh, sin, zeros], axis=1)
    tile2 = lambda a: jnp.concatenate([a, a], axis=1)
    return tile2(c), tile2(sa), tile2(sb)


_Q_SCALE = HD_B ** -0.5 * math.log2(math.e)
_ONES_ROWS = 16
_BOUND_SLACK = 1.01
_MIN_MASS = 2.0 ** -100
_HI16_MASK = -(2 ** 16)
_MIN_NORMAL16 = 0x0080
_LOW_BITS = 2
_MID_BITS = 14
_MID_MASK = 2 ** _MID_BITS - 1


def _rope128_t(x, c, sa, sb):
    return x * c + pltpu.roll(x, LANES - 8, 0) * sa + pltpu.roll(x, 8, 0) * sb


def _even_proj_body(x_ref, g_ref, w_ref, wwi_ref, cw_ref, cos_ref, sa_ref, sb_ref, init_ref,
                    ya_ref, q_ref, k_ref, v_ref, qi_ref, ki_ref, wit_ref, conv_ref, *, time_major_b):
    tm = x_ref.shape[0]
    xn = _rms(x_ref[...], g_ref[...]).astype(BF16)
    c, sa, sb = cos_ref[...], sa_ref[...], sb_ref[...]

    def seg(a, b):
        return _dot(xn, w_ref[:, a:b])

    z = seg(_EV_GC, _EV_Q) * seg(_EV_U, _EV_GB)
    cw = cw_ref[...]
    nb = time_major_b
    init = init_ref[...]
    z1 = jnp.concatenate([init[nb:2 * nb], z[:tm - nb]], axis=0)
    z2 = jnp.concatenate([init, z[:tm - 2 * nb]], axis=0)
    conv_ref[...] = z[tm - 2 * nb:, :]
    conv = cw[0:1] * z2 + cw[1:2] * z1 + cw[2:3] * z
    ya_ref[...] = (seg(_EV_GB, _EV_GC) * conv).astype(BF16)

    for j in range(4):
        a = _EV_Q + j * LANES
        q_ref[:, j * LANES:(j + 1) * LANES] = (
            _rope128(seg(a, a + LANES), c, sa, sb) * _Q_SCALE).astype(BF16)
        a = _EV_QI + j * LANES
        qi_ref[:, j * LANES:(j + 1) * LANES] = (
            _rope128(seg(a, a + LANES), c, sa, sb) * IDX_DIM ** -0.5).astype(BF16)
    k_ref[...] = _rope128(seg(_EV_K, _EV_V), c, sa, sb)
    v_ref[...] = seg(_EV_V, _EV_QI)
    ki_ref[...] = _rope128(seg(_EV_KI, _EV_END), c, sa, sb)[:, :IDX_DIM]
    wit_ref[...] = _dot_nt(wwi_ref[...], xn) * N_IDX_HEADS ** -0.5


def _even_proj_prompt_body(x_ref, g_ref, w_ref, wt_ref, cw_ref, cos_ref, sa_ref, sb_ref,
                           cost_ref, sat_ref, sbt_ref, init_ref,
                           ya_ref, k_ref, v_ref, ki_ref, kb_ref, kib_ref, qt_ref, qit_ref, vt_ref,
                           wit_ref, conv_ref, carry_ref, *, tiles_per_seq):
    tm = x_ref.shape[0]
    xn = _rms(x_ref[...], g_ref[...]).astype(BF16)
    c, sa, sb = cos_ref[...], sa_ref[...], sb_ref[...]
    ct, sat, sbt = cost_ref[...], sat_ref[...], sbt_ref[...]

    def seg(a, b):
        return _dot(xn, w_ref[:, a:b])

    z = seg(_EV_GC, _EV_Q) * seg(_EV_U, _EV_GB)
    cw = cw_ref[...]

    @pl.when(pl.program_id(0) % tiles_per_seq == 0)
    def _():
        carry_ref[0:2, :] = init_ref[0]

    c0 = carry_ref[0:1, :]
    c1 = carry_ref[1:2, :]
    row = lax.broadcasted_iota(I32, (tm, 1), 0)
    z1 = jnp.where(row == 0, c1, pltpu.roll(z, 1, 0))
    z2 = jnp.where(row == 0, c0, jnp.where(row == 1, c1, pltpu.roll(z, 2, 0)))
    carry_ref[0:2, :] = z[tm - 2:tm, :]
    conv_ref[0] = z[tm - 2:tm, :]
    conv = cw[0:1] * z2 + cw[1:2] * z1 + cw[2:3] * z
    ya_ref[...] = (seg(_EV_GB, _EV_GC) * conv).astype(BF16)

    k = _rope128(seg(_EV_K, _EV_V), c, sa, sb)
    k_ref[...] = k
    kb_ref[...] = k.astype(BF16)
    v_ref[...] = seg(_EV_V, _EV_QI)
    ki = _rope128(seg(_EV_KI, _EV_END), c, sa, sb)[:, :IDX_DIM]
    ki_ref[...] = ki
    kib_ref[...] = ki.astype(BF16)

    nq = N_HEADS_B * HD_B
    nqi = N_IDX_HEADS * IDX_DIM
    allt = _dot_nt(wt_ref[...], xn)
    for j in range(nq // LANES):
        xt = allt[j * LANES:(j + 1) * LANES]
        qt_ref[j * LANES:(j + 1) * LANES, :] = (_rope128_t(xt, ct, sat, sbt) * _Q_SCALE).astype(BF16)
    for j in range(nqi // LANES):
        xt = allt[nq + j * LANES:nq + (j + 1) * LANES]
        qit_ref[j * LANES:(j + 1) * LANES, :] = (
            _rope128_t(xt, ct, sat, sbt) * IDX_DIM ** -0.5).astype(BF16)
    nkv = N_KV_B * HD_B
    vt = allt[nq + nqi:nq + nqi + nkv].astype(BF16)
    kc = vt_ref.shape[3]
    ones = jnp.ones((_ONES_ROWS, kc), BF16)
    for j in range(tm // kc):
        for g in range(N_KV_B):
            vt_ref[j, g] = jnp.concatenate(
                [vt[g * HD_B:(g + 1) * HD_B, j * kc:(j + 1) * kc], ones], axis=0)
    wit_ref[...] = allt[nq + nqi + nkv:nq + nqi + nkv + N_IDX_HEADS] * N_IDX_HEADS ** -0.5


def _even_proj_prompt(x, g, w, wt, cw, tabs, tabs_t, init, *, tm, seq_len, kc):
    m = x.shape[0]
    tiles_per_seq = seq_len // tm
    row = lambda i: (i, 0)
    col = lambda i: (0, i)
    const = lambda i: (0, 0)
    seq = lambda i: (i // tiles_per_seq, 0, 0)
    nq = N_HEADS_B * HD_B
    nqi = N_IDX_HEADS * IDX_DIM
    nkv = N_KV_B * HD_B
    tab_spec = pl.BlockSpec((tm, LANES), lambda i: (i % tiles_per_seq, 0))
    tabt_spec = pl.BlockSpec((LANES, tm), lambda i: (0, i % tiles_per_seq))
    vrows = HD_B + _ONES_ROWS
    in_specs = [
        pl.BlockSpec((tm, D_MODEL), row), pl.BlockSpec((1, D_MODEL), const),
        pl.BlockSpec((D_MODEL, _EV_END), const), pl.BlockSpec(wt.shape, const),
        pl.BlockSpec((CONV_W, D_CONV), const),
        tab_spec, tab_spec, tab_spec, tabt_spec, tabt_spec, tabt_spec,
        pl.BlockSpec((1, 2, D_CONV), seq),
    ]
    out_specs = (
        pl.BlockSpec((tm, D_CONV), row),
        pl.BlockSpec((tm, nkv), row), pl.BlockSpec((tm, nkv), row), pl.BlockSpec((tm, IDX_DIM), row),
        pl.BlockSpec((tm, nkv), row), pl.BlockSpec((tm, IDX_DIM), row),
        pl.BlockSpec((nq, tm), col), pl.BlockSpec((nqi, tm), col),
        pl.BlockSpec((tm // kc, N_KV_B, vrows, kc), lambda i: (i, 0, 0, 0)),
        pl.BlockSpec((N_IDX_HEADS, tm), col),
        pl.BlockSpec((1, 2, D_CONV), seq),
    )
    out_shape = (
        jax.ShapeDtypeStruct((m, D_CONV), BF16),
        jax.ShapeDtypeStruct((m, nkv), F32), jax.ShapeDtypeStruct((m, nkv), F32),
        jax.ShapeDtypeStruct((m, IDX_DIM), F32),
        jax.ShapeDtypeStruct((m, nkv), BF16), jax.ShapeDtypeStruct((m, IDX_DIM), BF16),
        jax.ShapeDtypeStruct((nq, m), BF16), jax.ShapeDtypeStruct((nqi, m), BF16),
        jax.ShapeDtypeStruct((m // kc, N_KV_B, vrows, kc), BF16),
        jax.ShapeDtypeStruct((N_IDX_HEADS, m), F32),
        jax.ShapeDtypeStruct((m // seq_len, 2, D_CONV), F32),
    )
    return pl.pallas_call(
        functools.partial(_even_proj_prompt_body, tiles_per_seq=tiles_per_seq),
        grid=(m // tm,), in_specs=in_specs, out_specs=out_specs, out_shape=out_shape,
        scratch_shapes=[pltpu.VMEM((SUBLANES, D_CONV), F32)],
        compiler_params=_cparams(("arbitrary",)), name="even_proj_prompt",
    )(x, g, w, wt, cw, *tabs, *tabs_t, init)


def _even_proj(x, g, w, wwi, cw, tabs, init, *, seq_len, time_major_b):
    m = x.shape[0]
    tm = m
    nt = 1
    assert seq_len >= 2
    row = lambda i: (i, 0)
    const = lambda i: (0, 0)
    tab_spec = pl.BlockSpec((tm, LANES), const)
    init_spec = pl.BlockSpec((2 * time_major_b, D_CONV), const)
    conv_shape = jax.ShapeDtypeStruct((2 * time_major_b, D_CONV), F32)
    conv_spec = pl.BlockSpec((2 * time_major_b, D_CONV), const)
    scratch = []
    out_shape = (
        jax.ShapeDtypeStruct((m, D_CONV), BF16),
        jax.ShapeDtypeStruct((m, N_HEADS_B * HD_B), BF16),
        jax.ShapeDtypeStruct((m, N_KV_B * HD_B), F32),
        jax.ShapeDtypeStruct((m, N_KV_B * HD_B), F32),
        jax.ShapeDtypeStruct((m, N_IDX_HEADS * IDX_DIM), BF16),
        jax.ShapeDtypeStruct((m, IDX_DIM), F32),
        jax.ShapeDtypeStruct((N_IDX_HEADS, m), F32),
        conv_shape,
    )
    out_specs = (
        pl.BlockSpec((tm, D_CONV), row),
        pl.BlockSpec((tm, N_HEADS_B * HD_B), row),
        pl.BlockSpec((tm, N_KV_B * HD_B), row),
        pl.BlockSpec((tm, N_KV_B * HD_B), row),
        pl.BlockSpec((tm, N_IDX_HEADS * IDX_DIM), row),
        pl.BlockSpec((tm, IDX_DIM), row),
        pl.BlockSpec((N_IDX_HEADS, tm), lambda i: (0, i)),
        conv_spec,
    )
    in_specs = [
        pl.BlockSpec((tm, D_MODEL), row),
        pl.BlockSpec((1, D_MODEL), const),
        pl.BlockSpec((D_MODEL, _EV_END), const),
        pl.BlockSpec((N_IDX_HEADS, D_MODEL), const),
        pl.BlockSpec((CONV_W, D_CONV), const),
        tab_spec, tab_spec, tab_spec,
        init_spec,
    ]
    return pl.pallas_call(
        functools.partial(_even_proj_body, time_major_b=time_major_b),
        grid=(nt,), in_specs=in_specs, out_specs=out_specs, out_shape=out_shape,
        scratch_shapes=scratch, compiler_params=_cparams(("arbitrary",)),
        name="even_proj",
    )(x, g, w, wwi, cw, *tabs, init)


def _mix_out_body(*refs, n_in):
    ins = refs[:n_in]
    ws = refs[n_in:2 * n_in]
    x_ref, g1_ref, g2_ref, wq_ref, x1_ref, qx_ref = refs[2 * n_in:]
    y = _dot(ins[0][...], ws[0][...])
    for a, w in zip(ins[1:], ws[1:]):
        y = y + _dot(a[...], w[...])
    x1 = x_ref[...] + _rms(y, g1_ref[...])
    x1_ref[...] = x1
    xn = _rms(x1, g2_ref[...]).astype(BF16)
    qx_ref[...] = (_dot(xn, wq_ref[...]) * HD_X ** -0.5).astype(BF16)


def _mix_out(ins, ws, x, g1, g2, wq, *, tm):
    m = x.shape[0]
    row = lambda i: (i, 0)
    const = lambda i: (0, 0)
    n_in = len(ins)
    in_specs = ([pl.BlockSpec((tm, a.shape[1]), row) for a in ins]
                + [pl.BlockSpec(w.shape, const) for w in ws]
                + [pl.BlockSpec((tm, D_MODEL), row), pl.BlockSpec((1, D_MODEL), const),
                   pl.BlockSpec((1, D_MODEL), const), pl.BlockSpec(wq.shape, const)])
    nq = wq.shape[1]
    return pl.pallas_call(
        functools.partial(_mix_out_body, n_in=n_in),
        grid=(m // tm,), in_specs=in_specs,
        out_specs=(pl.BlockSpec((tm, D_MODEL), row), pl.BlockSpec((tm, nq), row)),
        out_shape=(jax.ShapeDtypeStruct((m, D_MODEL), F32), jax.ShapeDtypeStruct((m, nq), BF16)),
        compiler_params=_cparams(("parallel",)), name="mix_out",
    )(*ins, *ws, x, g1, g2, wq)


def _mem_kv_body(mem_ref, g_ref, wk_ref, wv_ref, mk_ref, mv_ref):
    mn = _rms(mem_ref[...], g_ref[0]).astype(BF16)
    mk_ref[0] = _dot(mn, wk_ref[0])
    mv_ref[0] = _dot(mn, wv_ref[0])


def _mem_kv(mem, g, wk, wv, *, tm):
    m = mem.shape[0]
    depth = g.shape[0]
    n = wk.shape[2]
    wspec = pl.BlockSpec((1, D_MODEL, n), lambda l, i: (l, 0, 0))
    ospec = pl.BlockSpec((1, tm, n), lambda l, i: (l, i, 0))
    oshape = jax.ShapeDtypeStruct((depth, m, n), F32)
    return pl.pallas_call(
        _mem_kv_body, grid=(depth, m // tm),
        in_specs=[pl.BlockSpec((tm, D_MODEL), lambda l, i: (i, 0)),
                  pl.BlockSpec((1, 1, D_MODEL), lambda l, i: (l, 0, 0)), wspec, wspec],
        out_specs=(ospec, ospec), out_shape=(oshape, oshape),
        compiler_params=_cparams(("parallel", "parallel")), name="mem_kv",
    )(mem, g, wk, wv)


def _xattn_body(q_ref, mk_ref, mv_ref, x_ref, wo_ref, g_ref, o_ref):
    bb, tm = q_ref.shape[0], q_ref.shape[1]
    rows = []
    for j in range(bb):
        q = q_ref[j]
        mk = mk_ref[j].astype(BF16)
        mv = mv_ref[j].astype(BF16)
        outs = []
        for h in range(N_HEADS_X):
            sl = slice(h * HD_X, (h + 1) * HD_X)
            s = _dot_nt(q[:, sl], mk[:, sl])
            s = s - jnp.max(s, axis=-1, keepdims=True)
            p = jnp.exp(s)
            p = p / jnp.sum(p, axis=-1, keepdims=True)
            outs.append(_dot(p.astype(BF16), mv[:, sl]))
        rows.append(jnp.concatenate(outs, axis=-1).astype(BF16))
    o = rows[0] if bb == 1 else jnp.concatenate(rows, axis=0)
    x = x_ref[...].reshape(bb * tm, D_MODEL)
    o_ref[...] = (x + _rms(_dot(o, wo_ref[...]), g_ref[...])).reshape(bb, tm, D_MODEL)


def _xattn(q, mk, mv, x, wo, g, *, tm, bb):
    b, t, _ = x.shape
    nq = q.shape[2]
    tile = lambda i, j: (i, j, 0)
    per_b = lambda i, j: (i, 0, 0)
    const = lambda i, j: (0, 0)
    return pl.pallas_call(
        _xattn_body, grid=(b // bb, t // tm),
        in_specs=[pl.BlockSpec((bb, tm, nq), tile),
                  pl.BlockSpec((bb, N_MEM, nq), per_b), pl.BlockSpec((bb, N_MEM, nq), per_b),
                  pl.BlockSpec((bb, tm, D_MODEL), tile),
                  pl.BlockSpec(wo.shape, const), pl.BlockSpec((1, D_MODEL), const)],
        out_specs=pl.BlockSpec((bb, tm, D_MODEL), tile),
        out_shape=jax.ShapeDtypeStruct((b, t, D_MODEL), F32),
        compiler_params=_cparams(("parallel", "parallel")), name="xattn",
    )(q, mk, mv, x, wo, g)


def _mlp_body(x_ref, g4_ref, g5_ref, w1_ref, w2_ref, o_ref):
    x = x_ref[...]
    xn = _rms(x, g4_ref[...]).astype(BF16)
    h = jnp.maximum(_dot(xn, w1_ref[...]), 0.0)
    y = _dot((h * h).astype(BF16), w2_ref[...])
    o_ref[...] = x + _rms(y, g5_ref[...])


def _mlp(x, g4, g5, w1, w2, *, tm):
    m = x.shape[0]
    row = lambda i: (i, 0)
    const = lambda i: (0, 0)
    resident = lambda a: pl.BlockSpec(a.shape, const, pipeline_mode=pl.Buffered(1))
    return pl.pallas_call(
        _mlp_body, grid=(m // tm,),
        in_specs=[pl.BlockSpec((tm, D_MODEL), row), pl.BlockSpec((1, D_MODEL), const),
                  pl.BlockSpec((1, D_MODEL), const), resident(w1), resident(w2)],
        out_specs=pl.BlockSpec((tm, D_MODEL), row),
        out_shape=jax.ShapeDtypeStruct((m, D_MODEL), F32),
        compiler_params=_cparams(("parallel",)), name="mlp",
    )(x, g4, g5, w1, w2)


def _for_pairs(n, body):
    def pair(j, carry):
        body(2 * j, carry)
        body(2 * j + 1, carry)
        return carry

    lax.fori_loop(0, n // 2, pair, 0)

    @pl.when(n % 2 == 1)
    def _():
        body(n - 1, 0)


def _mid_bf16(v):
    return pltpu.bitcast(jnp.left_shift(v + _MIN_NORMAL16, 16), F32).astype(BF16)


def _tree_sum(parts):
    while len(parts) > 1:
        nxt = [a + b for a, b in zip(parts[::2], parts[1::2])]
        if len(parts) % 2:
            nxt.append(parts[-1])
        parts = nxt
    return parts[0]


def _sortable_key(score):
    bits = pltpu.bitcast(score, I32)
    return bits ^ ((bits >> 31) & 0x7FFFFFFF)


def _dsa_prompt_body(qit_ref, wit_ref, ki_ref, qt_ref, k_ref, vt_ref, o_ref,
                     key_ref, hi_ref, mid_ref, cls_ref, qpad_ref, m_ref, acc_ref, knorm_ref,
                     *, ktop, idx_bits):
    tq = qit_ref.shape[1]
    kc = tq
    i = pl.program_id(1)
    nck = i + 1
    kiota = lax.broadcasted_iota(I32, (kc, tq), 0)
    qidx = i * tq + lax.broadcasted_iota(I32, (kc, tq), 1)
    ng = N_HEADS_B // N_KV_B

    def chunk_off(c):
        return pl.multiple_of(c * kc, kc)

    def score_chunk(c, carry):
        off = chunk_off(c)
        kic = ki_ref[pl.ds(off, kc), :]
        sc = jnp.zeros((kc, tq), F32)
        for h in range(N_IDX_HEADS):
            s = _dot(kic, qit_ref[h * IDX_DIM:(h + 1) * IDX_DIM, :])
            sc = sc + wit_ref[h:h + 1, :] * jnp.maximum(s, 0.0)
        sc = jnp.where(kiota + off <= qidx, sc, -jnp.inf)
        bits = pltpu.bitcast(sc, I32)
        key = bits ^ ((bits >> 31) & 0x7FFFFFFF)
        key_ref[pl.ds(off, kc), :] = key
        hi_ref[pl.ds(off, kc), :] = pltpu.bitcast(bits & _HI16_MASK, F32).astype(BF16)
        mid_ref[pl.ds(off, kc), :] = _mid_bf16((key >> _LOW_BITS) & _MID_MASK)
        return carry

    _for_pairs(nck, score_chunk)

    @pl.when(nck % 2 == 1)
    def _():
        key_ref[pl.ds(chunk_off(nck), kc), :] = jnp.full((kc, tq), INT_MIN, I32)
        hi_ref[pl.ds(chunk_off(nck), kc), :] = jnp.full((kc, tq), -jnp.inf, BF16)
        mid_ref[pl.ds(chunk_off(nck), kc), :] = jnp.zeros((kc, tq), BF16)

    def count(pred):
        def one(c, acc):
            off = chunk_off(c)
            hit = pred(key_ref[pl.ds(off, kc), :], kiota + off).astype(I32)
            return acc + jnp.sum(hit.reshape(kc // SUBLANES, SUBLANES, tq), axis=0)

        def body(j, acc):
            return one(2 * j + 1, one(2 * j, acc))

        acc = lax.fori_loop(0, (nck + 1) // 2, body, jnp.zeros((SUBLANES, tq), I32))
        return jnp.sum(acc, axis=0, keepdims=True)

    rows16 = 2 * SUBLANES
    one16 = jnp.ones((), BF16)
    zero16 = jnp.zeros((), BF16)
    npair = (nck + 1) // 2

    def count_bf16(ref, cand_bf16):
        def one(c, acc):
            hit = jnp.where(ref[pl.ds(chunk_off(c), kc), :] >= cand_bf16, one16, zero16)
            return acc + _tree_sum([hit[r * rows16:(r + 1) * rows16] for r in range(kc // rows16)])

        def body(j, acc):
            return one(2 * j + 1, one(2 * j, acc))

        acc = lax.fori_loop(0, npair, body, jnp.zeros((rows16, tq), BF16))
        return jnp.sum(acc.astype(F32), axis=0, keepdims=True).astype(I32)

    def hi_bf16(key16):
        bits16 = (key16 ^ ((key16 >> 31) & 0x7FFF)) & 0xFFFF
        bits16 = jnp.where(jnp.logical_and(key16 >= 1, key16 < _MIN_NORMAL16), _MIN_NORMAL16, bits16)
        return pltpu.bitcast(jnp.left_shift(bits16, 16), F32).astype(BF16)

    def hi_step(j, carry):
        lo, cnt_lo = carry
        cand = lo + jnp.left_shift(jnp.int32(1), 15 - j)
        cnt = count_bf16(hi_ref, hi_bf16(cand))
        keep = cnt >= ktop
        return jnp.where(keep, cand, lo), jnp.where(keep, cnt, cnt_lo)

    hi16, cnt_hi = lax.fori_loop(
        0, 16, hi_step, (jnp.full((1, tq), -(2 ** 15), I32), jnp.full((1, tq), ktop, I32)))

    cnt_above = count_bf16(hi_ref, hi_bf16(hi16 + 1))
    cls = hi_bf16(hi16)

    def cls_chunk(c, carry):
        off = chunk_off(c)
        member = hi_ref[pl.ds(off, kc), :] == cls
        cls_ref[pl.ds(off, kc), :] = jnp.where(member, mid_ref[pl.ds(off, kc), :], zero16)
        return carry

    lax.fori_loop(0, 2 * npair, cls_chunk, 0)

    def mid_step(j, carry):
        lo, cnt_lo = carry
        cand = lo + jnp.left_shift(jnp.int32(1), _MID_BITS - 1 - j)
        cnt = cnt_above + count_bf16(cls_ref, _mid_bf16(cand))
        keep = cnt >= ktop
        return jnp.where(keep, cand, lo), jnp.where(keep, cnt, cnt_lo)

    mid, cnt_mid = lax.fori_loop(0, _MID_BITS, mid_step, (jnp.zeros((1, tq), I32), cnt_hi))

    def bit_step(j, carry):
        lo, cnt_lo = carry
        cand = lo + jnp.left_shift(jnp.int32(1), _LOW_BITS - 1 - j)
        cnt = count(lambda blk, _: blk >= cand)
        keep = cnt >= ktop
        return jnp.where(keep, cand, lo), jnp.where(keep, cnt, cnt_lo)

    thr, cnt_ge = lax.fori_loop(
        0, _LOW_BITS, bit_step,
        (jnp.left_shift(hi16, 16) + jnp.left_shift(mid, _LOW_BITS), cnt_mid))

    tie = jnp.logical_and(thr > KEY_NEG_INF, cnt_ge > ktop)

    @pl.when(jnp.max(tie.astype(I32)) > 0)
    def _():
        need = ktop - count(lambda blk, _: blk > thr)

        def idx_step(j, res):
            cand = res + jnp.left_shift(jnp.int32(1), idx_bits - 1 - j)
            cnt = count(lambda blk, kidx: jnp.logical_and(blk == thr, kidx < cand))
            return jnp.where(cnt < need, cand, res)

        jcut = lax.fori_loop(0, idx_bits, idx_step, jnp.zeros((1, tq), I32))

        def drop_chunk(c, carry):
            off = chunk_off(c)
            blk = key_ref[pl.ds(off, kc), :]
            drop = jnp.logical_and(blk == thr, kiota + off > jcut)
            key_ref[pl.ds(off, kc), :] = jnp.where(drop, blk - 1, blk)
            return carry

        lax.fori_loop(0, nck, drop_chunk, 0)

    thr_eff = jnp.maximum(thr, KEY_NEG_INF + 1)

    @pl.when(i == 0)
    def _():
        def knorm_chunk(c, mx):
            kk = k_ref[pl.ds(chunk_off(c), kc), :].astype(F32)
            sq = kk * kk
            per_g = [jnp.max(jnp.sum(sq[:, g * HD_B:(g + 1) * HD_B], axis=1, keepdims=True),
                             axis=0, keepdims=True) for g in range(N_KV_B)]
            return jnp.maximum(mx, jnp.concatenate(per_g, axis=0))

        mx = lax.fori_loop(0, k_ref.shape[0] // kc, knorm_chunk, jnp.zeros((N_KV_B, 1), F32))
        knorm_ref[...] = jnp.broadcast_to(mx, knorm_ref.shape)

    zero_half = jnp.zeros((HD_B, tq), BF16)
    for g in range(N_KV_B):
        cols, bounds = [], []
        for hh in range(ng):
            h = g * ng + hh
            qh = qt_ref[h * HD_B:(h + 1) * HD_B, :]
            halves = [zero_half] * N_KV_B
            halves[g] = qh
            cols.append(jnp.concatenate(halves, axis=0))
            qf = qh.astype(F32)
            qn2 = jnp.sum(qf * qf, axis=0, keepdims=True)
            bounds.append(jnp.sqrt(qn2 * knorm_ref[g:g + 1, 0:1]) * _BOUND_SLACK)
        qpad_ref[g, 0:N_KV_B * HD_B, :] = jnp.concatenate(cols, axis=1)
        neg_bound = jnp.broadcast_to(-jnp.concatenate(bounds, axis=1), (2 * SUBLANES, ng * tq))
        qpad_ref[g, N_KV_B * HD_B:N_KV_B * HD_B + 2 * SUBLANES, :] = jnp.where(
            lax.broadcasted_iota(I32, neg_bound.shape, 0) == 0, neg_bound, 0.0).astype(BF16)
    acc_ref[...] = jnp.zeros(acc_ref.shape, F32)
    ones_cols = jnp.ones((kc, 2 * SUBLANES), BF16)

    def chunk_inputs(c):
        off = chunk_off(c)
        bias = jnp.where(key_ref[pl.ds(off, kc), :] >= thr_eff, 0.0, NEG_BIG)
        return jnp.concatenate([bias] * ng, axis=1), k_ref[pl.ds(off, kc), :]

    def attn_chunk(c, carry):
        bias, kch = chunk_inputs(c)
        kaug = jnp.concatenate([kch, ones_cols], axis=1)
        for g in range(N_KV_B):
            p = jnp.exp2(_dot(kaug, qpad_ref[g]) + bias).astype(BF16)
            acc_ref[g] += _dot(vt_ref[c, g], p)
        return carry

    _for_pairs(nck, attn_chunk)

    @pl.when(jnp.min(acc_ref[:, HD_B:HD_B + 1, :]) < _MIN_MASS)
    def _():
        m_ref[...] = jnp.full(m_ref.shape, NEG_BIG, F32)
        acc_ref[...] = jnp.zeros(acc_ref.shape, F32)

        def exact_chunk(c, carry):
            bias, kch = chunk_inputs(c)
            for g in range(N_KV_B):
                s = _dot(kch, qpad_ref[g, 0:N_KV_B * HD_B, :]) + bias
                m_old = m_ref[g]
                m_new = jnp.maximum(m_old, jnp.max(s, axis=0, keepdims=True))
                alpha = jnp.exp2(m_old - m_new)
                p = jnp.exp2(s - m_new).astype(BF16)
                acc_ref[g] = alpha * acc_ref[g] + _dot(vt_ref[c, g], p)
                m_ref[g] = m_new
            return carry

        lax.fori_loop(0, nck, exact_chunk, 0)

    for g in range(N_KV_B):
        acc = acc_ref[g]
        o = acc[:HD_B] / acc[HD_B:HD_B + 1]
        for hp in range(ng // 2):
            pair = jnp.concatenate([o[:, (2 * hp) * tq:(2 * hp + 1) * tq],
                                    o[:, (2 * hp + 1) * tq:(2 * hp + 2) * tq]], axis=0)
            lane0 = (g * ng + 2 * hp) * HD_B
            o_ref[:, lane0:lane0 + 2 * HD_B] = pair.T.astype(BF16)


def _dsa_prompt(qit, wit, kib, qt, kb, vt, *, b, t, tq):
    hd = qt.shape[0]
    nt = t // tq
    ktop = min(TOPK_MAX, t // 4)
    idx_bits = max(1, (t - 1).bit_length())
    assert t // (2 * SUBLANES) <= 256, "bf16 hit counters are exact only up to 256"
    ng = N_HEADS_B // N_KV_B
    vrows = vt.shape[2]
    col = lambda bi, i: (0, bi * nt + i)
    return pl.pallas_call(
        functools.partial(_dsa_prompt_body, ktop=ktop, idx_bits=idx_bits),
        grid=(b, nt),
        in_specs=[
            pl.BlockSpec((hd, tq), col),
            pl.BlockSpec((N_IDX_HEADS, tq), col),
            pl.BlockSpec((t, IDX_DIM), lambda bi, i: (bi, 0)),
            pl.BlockSpec((hd, tq), col),
            pl.BlockSpec((t, N_KV_B * HD_B), lambda bi, i: (bi, 0)),
            pl.BlockSpec((nt, N_KV_B, vrows, tq), lambda bi, i: (bi, 0, 0, 0)),
        ],
        out_specs=pl.BlockSpec((tq, hd), lambda bi, i: (bi * nt + i, 0)),
        out_shape=jax.ShapeDtypeStruct((b * t, hd), BF16),
        scratch_shapes=[pltpu.VMEM((t + tq, tq), I32),
                        pltpu.VMEM((t + tq, tq), BF16), pltpu.VMEM((t + tq, tq), BF16),
                        pltpu.VMEM((t + tq, tq), BF16),
                        pltpu.VMEM((N_KV_B, N_KV_B * HD_B + 2 * SUBLANES, ng * tq), BF16),
                        pltpu.VMEM((N_KV_B, 1, ng * tq), F32),
                        pltpu.VMEM((N_KV_B, vrows, ng * tq), F32),
                        pltpu.VMEM((N_KV_B, LANES), F32)],
        compiler_params=_cparams(("parallel", "arbitrary")), name="dsa_prompt",
    )(qit, wit, kib, qt, kb, vt)


_TP = SUBLANES
_NB = LANES
_PG_MAX = 16


def _dsa_sample_body(pt_ref, qi_ref, w_ref, kin_ref, q_ref, kn_ref, vn_ref,
                     ci_hbm, ck_hbm, cv_hbm, o_ref,
                     ibuf, kbuf, vbuf, key_ref, keyn_ref, isem, ksem, vsem,
                     *, n_pages, pg, t_new, ktop, idx_bits):
    b = pl.program_id(0)
    slot = b % 2
    ck = pg * PAGE_SIZE
    nch = n_pages // pg
    past = n_pages * PAGE_SIZE

    def idx_copy(sl, p, page):
        return pltpu.make_async_copy(ci_hbm.at[page], ibuf.at[sl, p], isem.at[sl])

    def k_copy(p, page):
        return pltpu.make_async_copy(ck_hbm.at[page], kbuf.at[p], ksem)

    def v_copy(p, page):
        return pltpu.make_async_copy(cv_hbm.at[page], vbuf.at[p], vsem)

    def issue_idx(bb, sl):
        def f(p, carry):
            idx_copy(sl, p, pt_ref[bb, p]).start()
            return carry

        lax.fori_loop(0, n_pages, f, 0)

    @pl.when(b == 0)
    def _():
        issue_idx(0, 0)

    def issue_kv(p, carry):
        page = pt_ref[b, p]
        k_copy(p, page).start()
        v_copy(p, page).start()
        return carry

    lax.fori_loop(0, n_pages, issue_kv, 0)

    @pl.when(b + 1 < pl.num_programs(0))
    def _():
        issue_idx(b + 1, 1 - slot)

    def wait_pages(hbm, buf, sem):
        pltpu.make_async_copy(hbm.at[pl.ds(0, n_pages)], buf, sem).wait()

    lane = lax.broadcasted_iota(I32, (_TP, ck), 1)
    lane_n = lax.broadcasted_iota(I32, (_TP, _NB), 1)
    row_n = lax.broadcasted_iota(I32, (_TP, _NB), 0)

    qi = qi_ref[0]
    nt = qi.shape[0] // N_IDX_HEADS

    def scores(kpt):
        n = kpt.shape[1]
        s = jnp.maximum(_dot(qi, kpt), 0.0) * w_ref[0]
        return jnp.sum(s.reshape(N_IDX_HEADS, nt, n), axis=0)

    def pages_t(buf, first):
        return jnp.concatenate([buf[first + p] for p in range(pg)], axis=1).astype(BF16)

    wait_pages(ci_hbm, ibuf.at[slot], isem.at[slot])

    real_row = lax.broadcasted_iota(I32, (_TP, ck), 0) < t_new

    def sc_chunk(c, carry):
        sc = scores(pages_t(ibuf.at[slot], c * pg))
        key_ref[c] = _sortable_key(jnp.where(real_row, sc, -jnp.inf))
        return carry

    lax.fori_loop(0, nch, sc_chunk, 0)
    admissible = jnp.logical_and(lane_n <= row_n, row_n < t_new)
    keyn_ref[...] = _sortable_key(jnp.where(admissible, scores(kin_ref[0]), -jnp.inf))

    def count(pred):
        acc = jnp.zeros((_TP, ck), I32)
        for c in range(nch):
            acc = acc + pred(key_ref[c], lane + c * ck).astype(I32)
        acc_n = pred(keyn_ref[...], lane_n + past).astype(I32)
        return jnp.sum(acc, axis=1, keepdims=True) + jnp.sum(acc_n, axis=1, keepdims=True)

    def bit_step(j, carry):
        lo, cnt_lo = carry
        cand = lo + jnp.left_shift(jnp.int32(1), 31 - j)
        cnt = count(lambda blk, _: blk >= cand)
        keep = cnt >= ktop
        return jnp.where(keep, cand, lo), jnp.where(keep, cnt, cnt_lo)

    thr, cnt_ge = lax.fori_loop(
        0, 32, bit_step, (jnp.full((_TP, 1), INT_MIN, I32), jnp.full((_TP, 1), ktop, I32)))
    tie = jnp.logical_and(thr > KEY_NEG_INF, cnt_ge > ktop)

    @pl.when(jnp.max(tie.astype(I32)) > 0)
    def _():
        need = ktop - count(lambda blk, _: blk > thr)

        def idx_step(j, res):
            cand = res + jnp.left_shift(jnp.int32(1), idx_bits - 1 - j)
            cnt = count(lambda blk, kidx: jnp.logical_and(blk == thr, kidx < cand))
            return jnp.where(cnt < need, cand, res)

        jcut = lax.fori_loop(0, idx_bits, idx_step, jnp.zeros((_TP, 1), I32))

        def dropped(blk, kidx):
            return jnp.where(jnp.logical_and(blk == thr, kidx > jcut), blk - 1, blk)

        def drop_chunk(c, carry):
            key_ref[c] = dropped(key_ref[c], lane + c * ck)
            return carry

        lax.fori_loop(0, nch, drop_chunk, 0)
        keyn_ref[...] = dropped(keyn_ref[...], lane_n + past)

    thr_eff = jnp.maximum(thr, KEY_NEG_INF + 1)
    qp = q_ref[0]
    reps = qp.shape[0] // _TP

    def attend(keys, kct, vct, carry):
        m, l, acc = carry
        bias = jnp.where(keys >= thr_eff, 0.0, NEG_BIG)
        s = _dot(qp, kct) + jnp.concatenate([bias] * reps, axis=0)
        m_new = jnp.maximum(m, jnp.max(s, axis=-1, keepdims=True))
        alpha = jnp.exp2(m - m_new)
        p = jnp.exp2(s - m_new)
        l = alpha * l + jnp.sum(p, axis=-1, keepdims=True)
        acc = alpha * acc + _dot_nt(p.astype(BF16), vct)
        return m_new, l, acc

    wait_pages(ck_hbm, kbuf, ksem)
    wait_pages(cv_hbm, vbuf, vsem)

    def at_chunk(c, carry):
        return attend(key_ref[c], pages_t(kbuf, c * pg), pages_t(vbuf, c * pg), carry)

    nr = qp.shape[0]
    init = (jnp.full((nr, 1), NEG_BIG, F32), jnp.zeros((nr, 1), F32),
            jnp.zeros((nr, N_KV_B * HD_B), F32))
    carry = lax.fori_loop(0, nch, at_chunk, init)
    m, l, acc = attend(keyn_ref[...], kn_ref[0], vn_ref[0], carry)
    o_ref[0] = acc / l


def _dsa_sample(page_table, qi, w, kin, qp, kn, vn, cache_i, cache_k, cache_v, *, t_new):
    b, n_pages = page_table.shape
    pg = math.gcd(n_pages, _PG_MAX)
    nr = qp.shape[1]
    total = n_pages * PAGE_SIZE + t_new
    ktop = min(TOPK_MAX, total // 4)
    idx_bits = max(1, (total - 1).bit_length())
    nch = n_pages // pg
    per_b = lambda i, pt: (i, 0, 0)
    grid_spec = pltpu.PrefetchScalarGridSpec(
        num_scalar_prefetch=1, grid=(b,),
        in_specs=[pl.BlockSpec((1,) + qi.shape[1:], per_b), pl.BlockSpec((1,) + w.shape[1:], per_b),
                  pl.BlockSpec((1,) + kin.shape[1:], per_b), pl.BlockSpec((1,) + qp.shape[1:], per_b),
                  pl.BlockSpec((1,) + kn.shape[1:], per_b), pl.BlockSpec((1,) + vn.shape[1:], per_b),
                  pl.BlockSpec(memory_space=pl.ANY), pl.BlockSpec(memory_space=pl.ANY),
                  pl.BlockSpec(memory_space=pl.ANY)],
        out_specs=pl.BlockSpec((1, nr, N_KV_B * HD_B), per_b),
        scratch_shapes=[pltpu.VMEM((2, n_pages, IDX_DIM, PAGE_SIZE), F32),
                        pltpu.VMEM((n_pages, N_KV_B * HD_B, PAGE_SIZE), F32),
                        pltpu.VMEM((n_pages, N_KV_B * HD_B, PAGE_SIZE), F32),
                        pltpu.VMEM((nch, _TP, pg * PAGE_SIZE), I32),
                        pltpu.VMEM((_TP, _NB), I32),
                        pltpu.SemaphoreType.DMA((2,)),
                        pltpu.SemaphoreType.DMA(()),
                        pltpu.SemaphoreType.DMA(())])
    return pl.pallas_call(
        functools.partial(_dsa_sample_body, n_pages=n_pages, pg=pg, t_new=t_new, ktop=ktop,
                          idx_bits=idx_bits),
        grid_spec=grid_spec,
        out_shape=jax.ShapeDtypeStruct((b, nr, N_KV_B * HD_B), F32),
        compiler_params=_cparams(("arbitrary",)), name="dsa_sample",
    )(page_table, qi, w, kin, qp, kn, vn, cache_i, cache_k, cache_v)


def _log_sigmoid(x):
    return jnp.minimum(x, 0.0) - jnp.log1p(jnp.exp(-jnp.abs(x)))


def _odd_proj_body(x_ref, g_ref, w_ref, wt_ref, wg_ref, bg_ref, bgt_ref,
                   qt_ref, k_ref, vt_ref, so_ref, gate_ref, gatet_ref):
    xn = _rms(x_ref[...], g_ref[...]).astype(BF16)
    nqk = N_HEADS_C * DQK_C
    k_ref[...] = _dot(xn, w_ref[:, :nqk]).astype(BF16)
    so_ref[...] = jax.nn.sigmoid(_dot(xn, w_ref[:, nqk:])).astype(BF16)
    nv = N_HEADS_C * DV_C
    allt = _dot_nt(wt_ref[...], xn)
    qt_ref[...] = (allt[:nqk] * DQK_C ** -0.5).astype(BF16)
    vt_ref[...] = allt[nqk:nqk + nv].astype(BF16)
    nh = N_HEADS_C
    gate = _dot(xn, wg_ref[...]) + bg_ref[...]
    col = lax.broadcasted_iota(I32, gate.shape, 1)
    gate_ref[...] = jnp.where(col < nh, gate, _log_sigmoid(gate))
    gatet = allt[nqk + nv:] + bgt_ref[...]
    rowi = lax.broadcasted_iota(I32, gatet.shape, 0)
    gatet_ref[...] = jnp.where(rowi < nh, gatet, _log_sigmoid(gatet))


def _odd_proj(x, g, w, wt, wg, bg, bgt, *, tm):
    m = x.shape[0]
    row = lambda i: (i, 0)
    col = lambda i: (0, i)
    const = lambda i: (0, 0)
    nqk = N_HEADS_C * DQK_C
    nv = N_HEADS_C * DV_C
    ng = 2 * N_HEADS_C
    return pl.pallas_call(
        _odd_proj_body, grid=(m // tm,),
        in_specs=[pl.BlockSpec((tm, D_MODEL), row), pl.BlockSpec((1, D_MODEL), const),
                  pl.BlockSpec(w.shape, const), pl.BlockSpec(wt.shape, const),
                  pl.BlockSpec((D_MODEL, ng), const), pl.BlockSpec((1, ng), const),
                  pl.BlockSpec((ng, 1), const)],
        out_specs=(pl.BlockSpec((nqk, tm), col), pl.BlockSpec((tm, nqk), row),
                   pl.BlockSpec((nv, tm), col), pl.BlockSpec((tm, nv), row),
                   pl.BlockSpec((tm, ng), row), pl.BlockSpec((ng, tm), col)),
        out_shape=(jax.ShapeDtypeStruct((nqk, m), BF16), jax.ShapeDtypeStruct((m, nqk), BF16),
                   jax.ShapeDtypeStruct((nv, m), BF16), jax.ShapeDtypeStruct((m, nv), BF16),
                   jax.ShapeDtypeStruct((m, ng), F32), jax.ShapeDtypeStruct((ng, m), F32)),
        compiler_params=_cparams(("parallel",)), name="odd_proj",
    )(x, g, w, wt, wg, bg, bgt)


def _split3(x):
    hi = x.astype(BF16)
    r1 = x - hi.astype(F32)
    mid = r1.astype(BF16)
    lo = (r1 - mid.astype(F32)).astype(BF16)
    return hi, mid, lo


def _mlstm_body(qt_ref, k_ref, vt_ref, so_ref, gate_ref, gatet_ref, hg_ref,
                c0_ref, n0_ref, m0_ref,
                h_ref, c_ref, n_ref, m_ref, cs_ref, ns_ref, ms_ref):
    lc = k_ref.shape[0]
    nh = N_HEADS_C
    c = pl.program_id(1)

    @pl.when(c == 0)
    def _():
        cs_ref[...] = c0_ref[0]
        ns_ref[...] = n0_ref[0]
        ms_ref[...] = jnp.broadcast_to(m0_ref[0], ms_ref.shape)

    ri = lax.broadcasted_iota(I32, (lc, lc), 0)
    ci = lax.broadcasted_iota(I32, (lc, lc), 1)
    causal_t = ri <= ci
    tril = (ci <= ri).astype(BF16)
    triu = causal_t.astype(BF16)
    gate = gate_ref[...]
    gatet = gatet_ref[...]
    lf3 = _split3(gate[:, nh:])
    bcols = _dot(tril, lf3[0]) + _dot(tril, lf3[1]) + _dot(tril, lf3[2])
    lft3 = _split3(gatet[nh:, :])
    brows = _dot(lft3[0], triu) + _dot(lft3[1], triu) + _dot(lft3[2], triu)
    half_lane = lax.broadcasted_iota(I32, (1, 2 * DQK_C), 1) // DQK_C
    zero_q = jnp.zeros((DQK_C, lc), BF16)
    ms = ms_ref[...]
    ms_new = []
    for h in range(nh):
        j, half = divmod(h, 2)
        in_half = half_lane == half
        kp = k_ref[:, j * 2 * DQK_C:(j + 1) * 2 * DQK_C]
        qh = qt_ref[h * DQK_C:(h + 1) * DQK_C, :]
        qpad = jnp.concatenate([qh, zero_q] if half == 0 else [zero_q, qh], axis=0)
        vt = vt_ref[h * DV_C:(h + 1) * DV_C, :]
        br = brows[h:h + 1, :]
        igr = gatet[h:h + 1, :]
        m_prev = ms[h:h + 1, 0:1]
        a = br + m_prev
        src = gate[:, h:h + 1] - bcols[:, h:h + 1]
        d = jnp.where(causal_t, br + src, NEG_BIG)
        mj = jnp.maximum(a, jnp.max(d, axis=0, keepdims=True))
        s = _dot(kp, qpad) * jnp.exp(d - mj)
        aw = jnp.exp(a - mj)
        cp = cs_ref[j]
        n8 = jnp.broadcast_to(ns_ref[j:j + 1, :], (SUBLANES, 2 * DQK_C)).astype(BF16)
        num = _dot(vt, s.astype(BF16)) + aw * _dot(cp.astype(BF16), qpad)
        den = jnp.sum(s, axis=0, keepdims=True) + aw * _dot(n8, qpad)[0:1]
        ht = num / jnp.maximum(jnp.abs(den), jnp.exp(-mj))
        ht = ht * lax.rsqrt(jnp.mean(ht * ht, axis=0, keepdims=True) + NORM_EPS)
        sl = slice(h * DV_C, (h + 1) * DV_C)
        h_ref[:, sl] = (ht.T * hg_ref[:, sl] * so_ref[:, sl].astype(F32)).astype(BF16)
        b_last = br[:, lc - 1:lc]
        g_row = b_last - br + igr
        m_new = jnp.maximum(b_last + m_prev, jnp.max(g_row, axis=-1, keepdims=True))
        gw = jnp.exp(g_row - m_new)
        decay = jnp.exp(b_last + m_prev - m_new)
        upd = _dot((vt.astype(F32) * gw).astype(BF16), kp)
        cs_ref[j] = jnp.where(in_half, decay * cp + upd, cp)
        gw8 = jnp.broadcast_to(gw, (SUBLANES, lc)).astype(BF16)
        n_old = ns_ref[j:j + 1, :]
        ns_ref[j:j + 1, :] = jnp.where(in_half, decay * n_old + _dot(gw8, kp)[0:1], n_old)
        ms_new.append(jnp.broadcast_to(m_new, (1, ms.shape[1])))
    ms_ref[...] = jnp.concatenate(ms_new, axis=0)

    @pl.when(c == pl.num_programs(1) - 1)
    def _():
        c_ref[0] = cs_ref[...]
        n_ref[0] = ns_ref[...]
        m_ref[0] = ms_ref[...]


def _mlstm(qt, k, vt, so, gate, gatet, hg, c0p, n0p, m0, *, b, t, lc):
    nh = N_HEADS_C
    nqk = nh * DQK_C
    nv = nh * DV_C
    ng = gate.shape[1]
    nc = t // lc
    npair = nh // 2
    row = lambda bi, c: (bi * nc + c, 0)
    col = lambda bi, c: (0, bi * nc + c)
    per_b4 = lambda bi, c: (bi, 0, 0, 0)
    per_b3 = lambda bi, c: (bi, 0, 0)
    return pl.pallas_call(
        _mlstm_body, grid=(b, nc),
        in_specs=[pl.BlockSpec((nqk, lc), col), pl.BlockSpec((lc, nqk), row),
                  pl.BlockSpec((nv, lc), col), pl.BlockSpec((lc, nv), row),
                  pl.BlockSpec((lc, ng), row), pl.BlockSpec((ng, lc), col),
                  pl.BlockSpec((1, nv), lambda bi, c: (0, 0)),
                  pl.BlockSpec((1, npair, DV_C, 2 * DQK_C), per_b4),
                  pl.BlockSpec((1, npair, 2 * DQK_C), per_b3),
                  pl.BlockSpec((1, nh, 1), per_b3)],
        out_specs=(pl.BlockSpec((lc, nv), row),
                   pl.BlockSpec((1, npair, DV_C, 2 * DQK_C), per_b4),
                   pl.BlockSpec((1, npair, 2 * DQK_C), per_b3),
                   pl.BlockSpec((1, nh, LANES), per_b3)),
        out_shape=(jax.ShapeDtypeStruct((b * t, nv), BF16),
                   jax.ShapeDtypeStruct((b, npair, DV_C, 2 * DQK_C), F32),
                   jax.ShapeDtypeStruct((b, npair, 2 * DQK_C), F32),
                   jax.ShapeDtypeStruct((b, nh, LANES), F32)),
        scratch_shapes=[pltpu.VMEM((npair, DV_C, 2 * DQK_C), F32),
                        pltpu.VMEM((npair, 2 * DQK_C), F32), pltpu.VMEM((nh, LANES), F32)],
        compiler_params=_cparams(("parallel", "arbitrary")), name="mlstm",
    )(qt, k, vt, so, gate, gatet, hg, c0p, n0p, m0)


def _prep_even(w_in, w_out):
    sizes = [D_CONV, D_CONV, D_CONV, N_HEADS_B * HD_B, N_KV_B * HD_B, N_KV_B * HD_B,
             N_IDX_HEADS * IDX_DIM, N_IDX_HEADS]
    offs = [sum(sizes[:j + 1]) for j in range(len(sizes))]
    u, gb, gc, q, k, v, qi, wi, ki = jnp.split(w_in, offs, axis=1)
    pad = jnp.zeros((D_MODEL, LANES - IDX_DIM), w_in.dtype)
    w = jnp.concatenate([u, gb, gc, q, k, v, qi, ki, pad], axis=1).astype(BF16)
    wpad = jnp.zeros((D_MODEL, 2 * SUBLANES - N_IDX_HEADS), w_in.dtype)
    wt = jnp.concatenate([q, qi, v, wi, wpad], axis=1).T.astype(BF16)
    return (w, wt, wi.T.astype(BF16), w_out[:D_CONV].astype(BF16), w_out[D_CONV:].astype(BF16))


def _prep_odd(w_in, b_i, b_f):
    q, k, v, o, wg = jnp.split(w_in, [_OD_K, _OD_V, _OD_O, _OD_END], axis=1)
    w = jnp.concatenate([k, o], axis=1).astype(BF16)
    wt = jnp.concatenate([q, v, wg], axis=1).T.astype(BF16)
    bg = jnp.concatenate([b_i, b_f])
    return w, wt, wg.astype(BF16), bg[None, :], bg[:, None]


def _tile(n, pref):
    return pref if n % pref == 0 else n


def _pad_axis(a, axis, n):
    if a.shape[axis] == n:
        return a
    widths = [(0, 0)] * a.ndim
    widths[axis] = (0, n - a.shape[axis])
    return jnp.pad(a, widths)


def _tail(x2, qx, mk, mv, wo, g3, g4, g5, w1, w2, *, b, t):
    t_pad = max(t, 2 * SUBLANES)
    tx = _tile(t_pad, 512)
    q3 = _pad_axis(qx.reshape(b, t, -1), 1, t_pad)
    x3 = _pad_axis(x2.reshape(b, t, D_MODEL), 1, t_pad)
    bb = 1 if tx >= LANES else math.gcd(b, LANES // tx)
    x3 = _xattn(q3, mk, mv, x3, wo, g3, tm=tx, bb=bb)[:, :t].reshape(b * t, D_MODEL)
    return _mlp(x3, g4, g5, w1, w2, tm=_tile(b * t, 512))


def _even_prompt(x2, g, we, cw, wxq, *, b, t, tm):
    w, wt, wwi, wo_a, wo_b = we
    tq = _tile(t, 256)
    tabs = _rope_tables(jnp.arange(t))
    tabs_t = tuple(tab.T for tab in tabs)
    init = jnp.zeros((b, CONV_W - 1, D_CONV), F32)
    ya, k, v, ki, kb, kib, qt, qit, vt, wit, conv = _even_proj_prompt(
        x2, g[0:1], w, wt, cw, tabs, tabs_t, init, tm=tm, seq_len=t, kc=tq)
    yb = _dsa_prompt(qit, wit, kib, qt, kb, vt, b=b, t=t, tq=tq)
    x1, qx = _mix_out([ya, yb], [wo_a, wo_b], x2, g[1:2], g[2:3], wxq, tm=tm)
    state = (k.reshape(b, t, N_KV_B, HD_B), v.reshape(b, t, N_KV_B, HD_B),
             ki.reshape(b, t, IDX_DIM), conv)
    return x1, qx, state


def _odd_mixer(x2, g, wod, hg, w_out, wxq, c0, n0, m0, *, b, t, tm, lc):
    w, wt, wg, bg, bgt = wod
    qt, k, vt, so, gate, gatet = _odd_proj(x2, g[0:1], w, wt, wg, bg, bgt, tm=tm)
    nh = N_HEADS_C
    tp = -(-t // lc) * lc
    if tp > t:
        rows = lambda a: _pad_axis(a.reshape(b, t, -1), 1, tp).reshape(b * tp, -1)
        cols = lambda a: _pad_axis(a.reshape(-1, b, t), 2, tp).reshape(-1, b * tp)
        neutral = jnp.concatenate([jnp.full((nh,), NEG_BIG, F32), jnp.zeros((nh,), F32)])
        gate = jnp.concatenate(
            [gate.reshape(b, t, 2 * nh),
             jnp.broadcast_to(neutral[None, None, :], (b, tp - t, 2 * nh))], axis=1)
        gatet = jnp.concatenate(
            [gatet.reshape(2 * nh, b, t),
             jnp.broadcast_to(neutral[:, None, None], (2 * nh, b, tp - t))], axis=2)
        qt, k, vt, so = cols(qt), rows(k), cols(vt), rows(so)
        gate, gatet = gate.reshape(b * tp, 2 * nh), gatet.reshape(2 * nh, b * tp)
    c0p = c0.reshape(b, nh // 2, 2, DV_C, DQK_C).transpose(0, 1, 3, 2, 4)
    c0p = c0p.reshape(b, nh // 2, DV_C, 2 * DQK_C)
    h, cp, n, m = _mlstm(qt, k, vt, so, gate, gatet, hg, c0p, n0.reshape(b, nh // 2, 2 * DQK_C),
                         m0[:, :, None], b=b, t=tp, lc=lc)
    h2 = h.reshape(b, tp, nh * DV_C)[:, :t].reshape(b * t, nh * DV_C)
    x1, qx = _mix_out([h2], [w_out], x2, g[1:2], g[2:3], wxq, tm=tm)
    c_out = cp.reshape(b, nh // 2, DV_C, 2, DQK_C).transpose(0, 1, 3, 2, 4)
    c_out = c_out.reshape(b, nh, DV_C, DQK_C)
    return x1, qx, (c_out, n.reshape(b, nh, DQK_C), m[:, :, 0])


def _even_sample(x2, g, we, cw, wxq, cache_i, cache_k, cache_v, state_conv, page_table, *, b, t):
    w, _, wwi, wo_a, wo_b = we
    n_pages = page_table.shape[1]
    pos = n_pages * PAGE_SIZE + jnp.arange(t)
    tabs = tuple(jnp.repeat(tab, b, axis=0) for tab in _rope_tables(pos))
    to_tm = lambda a: a.reshape(b, t, -1).transpose(1, 0, 2).reshape(t * b, -1)
    to_bm = lambda a: a.reshape(t, b, -1).transpose(1, 0, 2)
    init = state_conv.transpose(1, 0, 2).reshape(2 * b, D_CONV)
    ya, q, k, v, qi, ki, wit, conv = _even_proj(
        to_tm(x2), g[0:1], w, wwi, cw, tabs, init, seq_len=t, time_major_b=b)
    ng = N_HEADS_B // N_KV_B
    qi_b = to_bm(qi).reshape(b, t, N_IDX_HEADS, IDX_DIM).transpose(0, 2, 1, 3)
    qi_b = _pad_axis(qi_b, 2, _TP).reshape(b, N_IDX_HEADS * _TP, IDX_DIM)
    w_b = _pad_axis(wit.reshape(N_IDX_HEADS, t, b).transpose(2, 0, 1), 2, _TP)
    w_b = w_b.reshape(b, N_IDX_HEADS * _TP, 1)
    new_t = lambda a: _pad_axis(to_bm(a), 1, _NB).transpose(0, 2, 1).astype(BF16)
    kin, kn, vn = new_t(ki), new_t(k), new_t(v)
    q5 = _pad_axis(to_bm(q).reshape(b, t, N_KV_B, ng, HD_B).transpose(0, 2, 3, 1, 4), 3, _TP)
    zq = jnp.zeros_like(q5[:, 0])
    qp = jnp.stack([jnp.concatenate([q5[:, 0], zq], axis=-1),
                    jnp.concatenate([zq, q5[:, 1]], axis=-1)], axis=1)
    qp = qp.reshape(b, N_KV_B * ng * _TP, N_KV_B * HD_B)
    n_phys = cache_k.shape[0]
    pool_t = lambda a: a.reshape(n_phys, PAGE_SIZE, -1).transpose(0, 2, 1)
    o = _dsa_sample(page_table, qi_b, w_b, kin, qp, kn, vn,
                    pool_t(cache_i), pool_t(cache_k), pool_t(cache_v), t_new=t)
    o = o.reshape(b, N_KV_B, ng, _TP, N_KV_B, HD_B)[:, :, :, :t]
    yb = jnp.stack([o[:, 0, :, :, 0], o[:, 1, :, :, 1]], axis=1)
    yb = yb.transpose(3, 0, 1, 2, 4).reshape(t * b, N_HEADS_B * HD_B).astype(BF16)
    x1, qx = _mix_out([ya, yb], [wo_a, wo_b], to_tm(x2), g[1:2], g[2:3], wxq, tm=t * b)
    back = lambda a: to_bm(a).reshape(b * t, -1)
    state = (to_bm(k).reshape(b, t, N_KV_B, HD_B), to_bm(v).reshape(b, t, N_KV_B, HD_B),
             to_bm(ki), conv.reshape(2, b, D_CONV).transpose(1, 0, 2))
    return back(x1), back(qx), state


def kernel(x_prompt, x_sample, cache_k, cache_v, cache_idx_k, cache_mem_k, cache_mem_v, state_conv, state_C, state_n, state_m, page_table, mem_prompt, norm_g, w_in_even, conv_w, w_out_even, w_in_odd, b_i, b_f, hnorm_g, w_out_odd, mem_norm_g, w_xq, w_xk, w_xv, w_xo, w_ff1, w_ff2):
    bp, tp, _ = x_prompt.shape
    bs, ts, _ = x_sample.shape
    depth = norm_g.shape[0]
    tm_p = _tile(tp, 512)
    lc = _tile(tp, 256)
    nmem = mem_prompt.shape[1]
    mk_all, mv_all = _mem_kv(mem_prompt.reshape(bp * nmem, D_MODEL), mem_norm_g[:, None, :],
                             w_xk.astype(BF16), w_xv.astype(BF16), tm=_tile(bp * nmem, 512))
    xp = x_prompt.reshape(bp * tp, D_MODEL)
    xs = x_sample.reshape(bs * ts, D_MODEL)
    ev_p, ev_s, od_p, od_s = [], [], [], []
    for l in range(depth):
        g = norm_g[l]
        wxq = w_xq[l].astype(BF16)
        if l % 2 == 0:
            e = l // 2
            we = _prep_even(w_in_even[e], w_out_even[e])
            xp, qxp, st = _even_prompt(xp, g, we, conv_w[e], wxq, b=bp, t=tp, tm=tm_p)
            ev_p.append(st)
            xs, qxs, st = _even_sample(xs, g, we, conv_w[e], wxq, cache_idx_k[e], cache_k[e],
                                       cache_v[e], state_conv[e], page_table, b=bs, t=ts)
            ev_s.append(st)
        else:
            o = l // 2
            wod = _prep_odd(w_in_odd[o], b_i[o], b_f[o])
            hg = hnorm_g[o][None, :]
            wout = w_out_odd[o].astype(BF16)
            zc = jnp.zeros((bp, N_HEADS_C, DV_C, DQK_C), F32)
            zn = jnp.zeros((bp, N_HEADS_C, DQK_C), F32)
            zm = jnp.zeros((bp, N_HEADS_C), F32)
            xp, qxp, st = _odd_mixer(xp, g, wod, hg, wout, wxq, zc, zn, zm, b=bp, t=tp, tm=tm_p, lc=lc)
            od_p.append(st)
            xs, qxs, st = _odd_mixer(xs, g, wod, hg, wout, wxq, state_C[o], state_n[o], state_m[o],
                                     b=bs, t=ts, tm=bs * ts, lc=LANES)
            od_s.append(st)
        wxo = w_xo[l].astype(BF16)
        w1 = w_ff1[l].astype(BF16)
        w2 = w_ff2[l].astype(BF16)
        nx = N_HEADS_X * HD_X
        xp = _tail(xp, qxp, mk_all[l].reshape(bp, nmem, nx), mv_all[l].reshape(bp, nmem, nx),
                   wxo, g[3:4], g[4:5], g[5:6], w1, w2, b=bp, t=tp)
        xs = _tail(xs, qxs, cache_mem_k[l].reshape(bs, nmem, nx), cache_mem_v[l].reshape(bs, nmem, nx),
                   wxo, g[3:4], g[4:5], g[5:6], w1, w2, b=bs, t=ts)
    stack = lambda sts, j: jnp.stack([s[j] for s in sts])
    mem_shape = (depth, bp, nmem, N_HEADS_X, HD_X)
    return (xp.reshape(bp, tp, D_MODEL), xs.reshape(bs, ts, D_MODEL),
            stack(ev_p, 0), stack(ev_p, 1), stack(ev_p, 2), stack(ev_p, 3),
            stack(od_p, 0), stack(od_p, 1), stack(od_p, 2),
            mk_all.reshape(mem_shape), mv_all.reshape(mem_shape),
            stack(ev_s, 0), stack(ev_s, 1), stack(ev_s, 2), stack(ev_s, 3),
            stack(od_s, 0), stack(od_s, 1), stack(od_s, 2))
```
